```python
import jax, jax.numpy as jnp
from jax import lax
import numpy as np

D_MODEL = 1024
BATCH = 8
SEQ = 2048
DEPTH = 4
DEC_BATCH = 128
DEC_SEQ = 4
PAST_LEN = 8192
PAGE_SIZE = 128

HEAD_DIM = 64
N_Q_HEADS = 8
N_KV_HEADS = 2
Q_PER_KV = N_Q_HEADS // N_KV_HEADS
D_ATTN = N_Q_HEADS * HEAD_DIM
D_KV = N_KV_HEADS * HEAD_DIM
D_CONV = D_MODEL - D_ATTN
D_MIX = D_ATTN + D_CONV
IN_W = D_ATTN + 2 * D_KV + 2 * D_CONV
WINDOW = 128
Q_BLOCK = 128
CONV_K = 31
CONV_HIST = CONV_K - 1
D_FF = 4 * D_MODEL
D_PLE = 256
ROPE_THETA = 10000.0
EPS = 1e-6
NEG = -1e30

kernel_name = 'swa_sink_conformer_hybrid'


def rmsnorm(x, g):
    xf = x.astype(jnp.float32)
    y = xf * lax.rsqrt(jnp.mean(xf * xf, axis=-1, keepdims=True) + EPS) * g.astype(jnp.float32)
    return y.astype(x.dtype)


def layernorm(x, g, b):
    xf = x.astype(jnp.float32)
    mu = jnp.mean(xf, axis=-1, keepdims=True)
    xc = xf - mu
    var = jnp.mean(xc * xc, axis=-1, keepdims=True)
    y = xc * lax.rsqrt(var + EPS) * g.astype(jnp.float32) + b.astype(jnp.float32)
    return y.astype(x.dtype)


def rope(x, pos):
    half = HEAD_DIM // 2
    inv = 1.0 / (ROPE_THETA ** (jnp.arange(half, dtype=jnp.float32) / half))
    ang = pos.astype(jnp.float32)[:, None] * inv[None, :]
    cos = jnp.cos(ang)[None, :, None, :]
    sin = jnp.sin(ang)[None, :, None, :]
    xf = x.astype(jnp.float32)
    x1, x2 = xf[..., :half], xf[..., half:]
    return jnp.concatenate([x1 * cos - x2 * sin, x2 * cos + x1 * sin], axis=-1).astype(x.dtype)


def window_attention(q, k_ext, v_ext, sinks, pos0):
    B, T = q.shape[0], q.shape[1]
    qb = Q_BLOCK if T % Q_BLOCK == 0 else T
    nb = T // qb
    L = WINDOW + qb
    starts = jnp.arange(nb, dtype=jnp.int32) * qb
    idx = starts[:, None] + jnp.arange(L, dtype=jnp.int32)[None, :]
    kb = k_ext[:, idx]
    vb = v_ext[:, idx]
    qr = q.reshape(B, nb, qb, N_KV_HEADS, Q_PER_KV, HEAD_DIM)
    s = jnp.einsum('bnqkgd,bnlkd->bnkgql', qr, kb).astype(jnp.float32) * (HEAD_DIM ** -0.5)
    qpos = pos0 + starts[:, None] + jnp.arange(qb, dtype=jnp.int32)[None, :]
    kpos = pos0 - WINDOW + idx
    dist = qpos[:, :, None] - kpos[:, None, :]
    mask = (kpos[:, None, :] >= 0) & (dist >= 0) & (dist <= WINDOW)
    s = jnp.where(mask[None, :, None, None], s, NEG)
    sink = sinks.astype(jnp.float32).reshape(N_KV_HEADS, Q_PER_KV)[None, None, :, :, None, None]
    sink_col = jnp.broadcast_to(sink, s.shape[:-1] + (1,))
    pr = jax.nn.softmax(jnp.concatenate([s, sink_col], axis=-1), axis=-1)[..., :-1]
    o = jnp.einsum('bnkgql,bnlkd->bnqkgd', pr.astype(v_ext.dtype), vb)
    return o.reshape(B, T, D_ATTN)


def causal_depthwise(u_ext, w, b):
    out = lax.conv_general_dilated(u_ext, w[:, None, :], window_strides=(1,), padding='VALID',
                                   dimension_numbers=('NWC', 'WIO', 'NWC'), feature_group_count=D_CONV)
    return out + b


def layer(x, p, hist_k, hist_v, hist_u, pos0, g_mix, w_in, sinks, w_dw, b_dw, ln_g, ln_b, w_pw2,
          g_attn_out, g_conv_out, w_o, g_ffn, w_up, w_down, g_ple, w_ple_gate, w_ple):
    B, T = x.shape[0], x.shape[1]
    h = rmsnorm(x, g_mix)
    z = h @ w_in
    q, k, v, u_a, u_g = jnp.split(z, [D_ATTN, D_ATTN + D_KV, D_ATTN + 2 * D_KV, D_ATTN + 2 * D_KV + D_CONV], axis=-1)
    pos = pos0 + jnp.arange(T, dtype=jnp.int32)
    q = rope(q.reshape(B, T, N_Q_HEADS, HEAD_DIM), pos)
    k = rope(k.reshape(B, T, N_KV_HEADS, HEAD_DIM), pos)
    v = v.reshape(B, T, N_KV_HEADS, HEAD_DIM)
    k_ext = jnp.concatenate([hist_k.astype(k.dtype), k], axis=1)
    v_ext = jnp.concatenate([hist_v.astype(v.dtype), v], axis=1)
    a = window_attention(q, k_ext, v_ext, sinks, pos0)
    u = u_a * jax.nn.sigmoid(u_g)
    u_ext = jnp.concatenate([hist_u.astype(u.dtype), u], axis=1)
    c = causal_depthwise(u_ext, w_dw, b_dw)
    c = jax.nn.silu(layernorm(c, ln_g, ln_b)) @ w_pw2
    mix = jnp.concatenate([rmsnorm(a, g_attn_out), rmsnorm(c, g_conv_out)], axis=-1) @ w_o
    x = x + mix
    x = x + jnp.square(jax.nn.relu(rmsnorm(x, g_ffn) @ w_up)) @ w_down
    x = x + (p @ w_ple) * jax.nn.sigmoid(rmsnorm(x, g_ple) @ w_ple_gate)
    return x, k_ext[:, -WINDOW:], v_ext[:, -WINDOW:], u_ext[:, -CONV_HIST:]


def setup_inputs(seed: int = 0) -> dict:
    key = jax.random.key(seed)
    ks = jax.random.split(key, 24)
    f32 = jnp.float32

    def nrm(k, shape, scale):
        return jax.random.normal(k, shape, f32) * scale

    def gain(k, shape):
        return 1.0 + 0.02 * jax.random.normal(k, shape, f32)

    return {
        'x_prompt': nrm(ks[0], (BATCH, SEQ, D_MODEL), 1.0),
        'x_sample': nrm(ks[1], (DEC_BATCH, DEC_SEQ, D_MODEL), 1.0),
        'p_prompt': nrm(ks[2], (DEPTH, BATCH, SEQ, D_PLE), 1.0),
        'p_sample': nrm(ks[3], (DEPTH, DEC_BATCH, DEC_SEQ, D_PLE), 1.0),
        'cache_k': nrm(ks[4], (DEPTH, DEC_BATCH, WINDOW, N_KV_HEADS, HEAD_DIM), 1.0),
        'cache_v': nrm(ks[5], (DEPTH, DEC_BATCH, WINDOW, N_KV_HEADS, HEAD_DIM), 1.0),
        'state_conv': nrm(ks[6], (DEPTH, DEC_BATCH, CONV_HIST, D_CONV), 0.5),
        'g_mix': gain(ks[7], (DEPTH, D_MODEL)),
        'w_in': nrm(ks[8], (DEPTH, D_MODEL, IN_W), D_MODEL ** -0.5),
        'sinks': nrm(ks[9], (DEPTH, N_Q_HEADS), 0.5),
        'w_dw': nrm(ks[10], (DEPTH, CONV_K, D_CONV), CONV_K ** -0.5),
        'b_dw': nrm(ks[11], (DEPTH, D_CONV), 0.02),
        'ln_g': gain(ks[12], (DEPTH, D_CONV)),
        'ln_b': nrm(ks[13], (DEPTH, D_CONV), 0.02),
        'w_pw2': nrm(ks[14], (DEPTH, D_CONV, D_CONV), D_CONV ** -0.5),
        'g_attn_out': gain(ks[15], (DEPTH, D_ATTN)),
        'g_conv_out': gain(ks[16], (DEPTH, D_CONV)),
        'w_o': nrm(ks[17], (DEPTH, D_MIX, D_MODEL), D_MIX ** -0.5),
        'g_ffn': gain(ks[18], (DEPTH, D_MODEL)),
        'w_up': nrm(ks[19], (DEPTH, D_MODEL, D_FF), D_MODEL ** -0.5),
        'w_down': nrm(ks[20], (DEPTH, D_FF, D_MODEL), D_FF ** -0.5),
        'g_ple': gain(ks[21], (DEPTH, D_MODEL)),
        'w_ple_gate': nrm(ks[22], (DEPTH, D_MODEL, D_MODEL), D_MODEL ** -0.5),
        'w_ple': nrm(ks[23], (DEPTH, D_PLE, D_MODEL), D_PLE ** -0.5),
        'g_final': gain(jax.random.fold_in(key, 99), (D_MODEL,)),
    }


def reference(x_prompt, x_sample, p_prompt, p_sample, cache_k, cache_v, state_conv,
              g_mix, w_in, sinks, w_dw, b_dw, ln_g, ln_b, w_pw2, g_attn_out, g_conv_out, w_o,
              g_ffn, w_up, w_down, g_ple, w_ple_gate, w_ple, g_final):
    yp, ys = x_prompt, x_sample
    bp = x_prompt.shape[0]
    zero_k = jnp.zeros((bp, WINDOW, N_KV_HEADS, HEAD_DIM), x_prompt.dtype)
    zero_u = jnp.zeros((bp, CONV_HIST, D_CONV), x_prompt.dtype)
    kp_l, vp_l, up_l, ks_l, vs_l, us_l = [], [], [], [], [], []
    for i in range(DEPTH):
        wts = (g_mix[i], w_in[i], sinks[i], w_dw[i], b_dw[i], ln_g[i], ln_b[i], w_pw2[i],
               g_attn_out[i], g_conv_out[i], w_o[i], g_ffn[i], w_up[i], w_down[i],
               g_ple[i], w_ple_gate[i], w_ple[i])
        yp, kp, vp, up = layer(yp, p_prompt[i], zero_k, zero_k, zero_u, 0, *wts)
        ys, kk, vv, uu = layer(ys, p_sample[i], cache_k[i], cache_v[i], state_conv[i], PAST_LEN, *wts)
        kp_l.append(kp); vp_l.append(vp); up_l.append(up)
        ks_l.append(kk); vs_l.append(vv); us_l.append(uu)
    y_prompt = rmsnorm(yp, g_final)
    y_sample = rmsnorm(ys, g_final)
    return (y_prompt, y_sample, jnp.stack(kp_l), jnp.stack(vp_l), jnp.stack(up_l),
            jnp.stack(ks_l), jnp.stack(vs_l), jnp.stack(us_l))
```

```python
import functools

import jax
import jax.numpy as jnp
from jax import lax
from jax.experimental import pallas as pl
from jax.experimental.pallas import tpu as pltpu

F32 = jnp.float32
BF16 = jnp.bfloat16

HEAD_DIM = 64
N_Q_HEADS = 8
N_KV_HEADS = 2
Q_PER_KV = N_Q_HEADS // N_KV_HEADS
D_ATTN = N_Q_HEADS * HEAD_DIM
D_KV = N_KV_HEADS * HEAD_DIM
WINDOW = 128
CONV_K = 31
CONV_HIST = CONV_K - 1
PAST_LEN = 8192
ROPE_THETA = 10000.0
EPS = 1e-6
NEG = -1e30

LANES = 128
Q_BLOCK = 128
CONV_PAD = 32
CONV_ROWS = 64
FF_CHUNK = 512
SEQS_PER_STEP = 8
VMEM_LIMIT = 56 * 1024 * 1024


def _rmsnorm(x, g):
    ms = jnp.mean(x * x, axis=-1, keepdims=True)
    return x * lax.rsqrt(ms + EPS) * g


def _dot(a, b):
    return jnp.dot(a, b, preferred_element_type=F32)


def _swap_half_heads(x):
    lane = lax.broadcasted_iota(jnp.int32, x.shape, 1)
    first = (lane & 32) == 0
    return jnp.where(first, pltpu.roll(x, LANES - 32, 1), pltpu.roll(x, 32, 1))


def _rope(x, cos, sin):
    return x * cos + _swap_half_heads(x) * sin


def _in_proj(x, g_mix, w_in_ref, cos, sin):
    h = _rmsnorm(x, g_mix).astype(BF16)
    q = _dot(h, w_in_ref[:, 0:D_ATTN])
    kv = _dot(h, w_in_ref[:, D_ATTN:D_ATTN + 2 * D_KV])
    o = D_ATTN + 2 * D_KV
    ua = _dot(h, w_in_ref[:, o:o + 512])
    ug = _dot(h, w_in_ref[:, o + 512:o + 1024])
    scale = HEAD_DIM ** -0.5
    qs = [_rope(q[:, LANES * c:LANES * (c + 1)], cos, sin) * scale for c in range(D_ATTN // LANES)]
    k = _rope(kv[:, 0:D_KV], cos, sin)
    v = kv[:, D_KV:2 * D_KV]
    u = ua * jax.nn.sigmoid(ug)
    return qs, k, v, u


def _attention(q_blocks, segments, sink_of_head):
    rows = q_blocks[0].shape[0]
    lane = lax.broadcasted_iota(jnp.int32, (rows, LANES), 1)
    low = lane < HEAD_DIM
    q_hat = []
    for h in range(N_Q_HEADS):
        g = h // Q_PER_KV
        blk = q_blocks[h // 2]
        if (h % 2) != g:
            blk = pltpu.roll(blk, HEAD_DIM, 1)
        q_hat.append(jnp.where(low if g == 0 else ~low, blk, 0.0).astype(BF16))
    q_hat = jnp.concatenate(q_hat, axis=0)
    scores = [lax.dot_general(q_hat, k, (((1,), (1,)), ((), ())), preferred_element_type=F32)
              for k, _, _ in segments]
    probs = [[] for _ in segments]
    dens = []
    for h in range(N_Q_HEADS):
        sink = sink_of_head(h)
        masked = [jnp.where(m, s[h * rows:(h + 1) * rows], NEG) for s, (_, _, m) in zip(scores, segments)]
        mx = sink
        for sm in masked:
            mx = jnp.maximum(mx, jnp.max(sm, axis=-1, keepdims=True))
        den = jnp.exp(sink - mx)
        for i, sm in enumerate(masked):
            e = jnp.exp(sm - mx)
            den = den + jnp.sum(e, axis=-1, keepdims=True)
            probs[i].append(e.astype(BF16))
        dens.append(den)
    out = None
    for i, (_, v, _) in enumerate(segments):
        o = _dot(jnp.concatenate(probs[i], axis=0), v)
        out = o if out is None else out + o
    placed = []
    for h in range(N_Q_HEADS):
        g = h // Q_PER_KV
        o = out[h * rows:(h + 1) * rows] / dens[h]
        if (h % 2) != g:
            o = pltpu.roll(o, HEAD_DIM, 1)
        placed.append(o)
    return [jnp.where(low, placed[2 * c], placed[2 * c + 1]) for c in range(D_ATTN // LANES)]


def _post_mix(a, cpre, ln_g, ln_b, w_pw2_ref, g_ao, g_co, w_o_ref):
    mu = jnp.mean(cpre, axis=-1, keepdims=True)
    xc = cpre - mu
    var = jnp.mean(xc * xc, axis=-1, keepdims=True)
    y = xc * lax.rsqrt(var + EPS) * ln_g + ln_b
    y = y * jax.nn.sigmoid(y)
    c = _dot(y.astype(BF16), w_pw2_ref[...])
    an = _rmsnorm(a, g_ao).astype(BF16)
    cn = _rmsnorm(c, g_co).astype(BF16)
    return _dot(an, w_o_ref[0:D_ATTN, :]) + _dot(cn, w_o_ref[D_ATTN:, :])


def _prompt_mixer_kernel(layer, tm,
                         sinks_ref, x_ref, cos_ref, sin_ref, g_mix_ref, w_in_ref, w_dw_ref, b_dw_ref,
                         ln_g_ref, ln_b_ref, w_pw2_ref, g_ao_ref, g_co_ref, w_o_ref,
                         x_out_ref, k_tail_ref, v_tail_ref, u_tail_ref,
                         q_scr, k_scr, v_scr, u_scr, a_scr, c_scr):
    s_idx = pl.program_id(1)

    @pl.when(s_idx == 0)
    def _():
        k_scr[0:WINDOW, :] = jnp.zeros((WINDOW, D_KV), BF16)
        v_scr[0:WINDOW, :] = jnp.zeros((WINDOW, D_KV), BF16)
        for c in range(512 // LANES):
            u_scr[c, pl.ds(0, CONV_PAD, stride=2), :] = jnp.zeros((CONV_PAD, LANES), F32)

    x = x_ref[0]
    qs, k, v, u = _in_proj(x, g_mix_ref[...], w_in_ref, cos_ref[...], sin_ref[...])
    for c, qc in enumerate(qs):
        q_scr[:, LANES * c:LANES * (c + 1)] = qc
    k_scr[WINDOW:, :] = k.astype(BF16)
    v_scr[WINDOW:, :] = v.astype(BF16)
    for c in range(512 // LANES):
        u_scr[c, pl.ds(2 * CONV_PAD, tm, stride=2), :] = u[:, LANES * c:LANES * (c + 1)]
    k_tail_ref[0] = k[tm - WINDOW:, :]
    v_tail_ref[0] = v[tm - WINDOW:, :]
    u_tail_ref[0] = u[tm - CONV_PAD:, :]

    t0 = s_idx * tm

    def q_block(j, carry):
        r0 = pl.multiple_of(j * Q_BLOCK, Q_BLOCK)
        q_blocks = [q_scr[pl.ds(r0, Q_BLOCK), LANES * c:LANES * (c + 1)] for c in range(D_ATTN // LANES)]
        kb = k_scr[pl.ds(r0, Q_BLOCK + WINDOW), :]
        vb = v_scr[pl.ds(r0, Q_BLOCK + WINDOW), :]
        r = lax.broadcasted_iota(jnp.int32, (Q_BLOCK, Q_BLOCK + WINDOW), 0)
        l = lax.broadcasted_iota(jnp.int32, (Q_BLOCK, Q_BLOCK + WINDOW), 1)
        mask = (l >= r) & (l <= r + WINDOW) & (l + (t0 + r0 - WINDOW) >= 0)
        a_blocks = _attention(q_blocks, [(kb, vb, mask)], lambda h: sinks_ref[layer, h])
        for c, ab in enumerate(a_blocks):
            a_scr[pl.ds(r0, Q_BLOCK), LANES * c:LANES * (c + 1)] = ab
        return carry

    lax.fori_loop(0, tm // Q_BLOCK, q_block, 0)

    for r0 in range(0, tm, CONV_ROWS):
        for c in range(512 // LANES):
            cols = slice(LANES * c, LANES * (c + 1))
            acc = jnp.zeros((CONV_ROWS, LANES), F32)
            for j in range(CONV_K):
                lo = r0 + (CONV_PAD - CONV_HIST) + j
                acc = acc + u_scr[c, pl.ds(2 * lo, CONV_ROWS, stride=2), :] * w_dw_ref[j:j + 1, cols]
            c_scr[r0:r0 + CONV_ROWS, cols] = acc + b_dw_ref[:, cols]

    mix = _post_mix(a_scr[...], c_scr[...], ln_g_ref[...], ln_b_ref[...], w_pw2_ref,
                    g_ao_ref[...], g_co_ref[...], w_o_ref)
    x_out_ref[0] = x + mix

    k_scr[0:WINDOW, :] = k_scr[tm:tm + WINDOW, :]
    v_scr[0:WINDOW, :] = v_scr[tm:tm + WINDOW, :]
    for c in range(512 // LANES):
        u_scr[c, pl.ds(0, CONV_PAD, stride=2), :] = u_scr[c, pl.ds(2 * tm, CONV_PAD, stride=2), :]


def _resident(shape, index_map):
    return pl.BlockSpec(shape, index_map, pipeline_mode=pl.Buffered(1))


def _prompt_mixer(layer, x, cos, sin, w, tm):
    b, s, d = x.shape
    n_s = s // tm
    lay = lambda *_: (layer, 0, 0)
    vec = lambda n: _resident((None, 1, n), lay)
    in_specs = [
        pl.BlockSpec(memory_space=pltpu.SMEM),
        pl.BlockSpec((1, tm, d), lambda bi, si: (bi, si, 0)),
        pl.BlockSpec((tm, LANES), lambda bi, si: (si, 0)),
        pl.BlockSpec((tm, LANES), lambda bi, si: (si, 0)),
        vec(d),
        _resident((None,) + w['w_in'].shape[1:], lay),
        _resident((None, CONV_K, 512), lay),
        vec(512), vec(512), vec(512),
        _resident((None, 512, 512), lay),
        vec(512), vec(512),
        _resident((None, d, d), lay),
    ]
    out_shape = [
        jax.ShapeDtypeStruct((b, s, d), F32),
        jax.ShapeDtypeStruct((b, WINDOW, D_KV), F32),
        jax.ShapeDtypeStruct((b, WINDOW, D_KV), F32),
        jax.ShapeDtypeStruct((b, CONV_PAD, 512), F32),
    ]
    out_specs = [
        pl.BlockSpec((1, tm, d), lambda bi, si: (bi, si, 0)),
        pl.BlockSpec((1, WINDOW, D_KV), lambda bi, si: (bi, 0, 0)),
        pl.BlockSpec((1, WINDOW, D_KV), lambda bi, si: (bi, 0, 0)),
        pl.BlockSpec((1, CONV_PAD, 512), lambda bi, si: (bi, 0, 0)),
    ]
    scratch = [
        pltpu.VMEM((tm, D_ATTN), F32),
        pltpu.VMEM((tm + WINDOW, D_KV), BF16),
        pltpu.VMEM((tm + WINDOW, D_KV), BF16),
        pltpu.VMEM((512 // LANES, 2 * (tm + CONV_PAD), LANES), F32),
        pltpu.VMEM((tm, D_ATTN), F32),
        pltpu.VMEM((tm, 512), F32),
    ]
    return pl.pallas_call(
        functools.partial(_prompt_mixer_kernel, layer, tm),
        grid=(b, n_s),
        in_specs=in_specs, out_specs=out_specs, out_shape=out_shape, scratch_shapes=scratch,
        compiler_params=pltpu.CompilerParams(
            dimension_semantics=("parallel", "arbitrary"), vmem_limit_bytes=VMEM_LIMIT),
        name=f"prompt_mixer_l{layer}",
    )(w['sinks'], x, cos, sin, w['g_mix'], w['w_in'], w['w_dw'], w['b_dw'], w['ln_g'], w['ln_b'],
      w['w_pw2'], w['g_attn_out'], w['g_conv_out'], w['w_o'])


def _ffn_ple_kernel(with_mix, final, *refs):
    refs = list(refs)
    x_ref = refs.pop(0)
    if with_mix:
        a_ref, cpre_ref, ln_g_ref, ln_b_ref, w_pw2_ref, g_ao_ref, g_co_ref, w_o_ref = refs[:8]
        refs = refs[8:]
    p_ref, g_ffn_ref, w_up_ref, w_down_ref, g_ple_ref, w_gate_ref, w_ple_ref = refs[:7]
    refs = refs[7:]
    if final:
        g_final_ref = refs.pop(0)
    (out_ref,) = refs

    x = x_ref[...]
    if with_mix:
        x = x + _post_mix(a_ref[...], cpre_ref[...], ln_g_ref[...], ln_b_ref[...], w_pw2_ref,
                          g_ao_ref[...], g_co_ref[...], w_o_ref)
    xn = _rmsnorm(x, g_ffn_ref[...]).astype(BF16)
    d_ff = w_up_ref.shape[1]
    acc = None
    for c in range(d_ff // FF_CHUNK):
        hc = _dot(xn, w_up_ref[:, FF_CHUNK * c:FF_CHUNK * (c + 1)])
        hc = jnp.square(jnp.maximum(hc, 0.0)).astype(BF16)
        part = _dot(hc, w_down_ref[FF_CHUNK * c:FF_CHUNK * (c + 1), :])
        acc = part if acc is None else acc + part
    x = x + acc
    gate = jax.nn.sigmoid(_dot(_rmsnorm(x, g_ple_ref[...]).astype(BF16), w_gate_ref[...]))
    x = x + _dot(p_ref[...].astype(BF16), w_ple_ref[...]) * gate
    if final:
        x = _rmsnorm(x, g_final_ref[...])
    out_ref[...] = x


def _ffn_ple(layer, x, p, w, tm, mix=None, g_final=None):
    t, d = x.shape
    lay = lambda *_: (layer, 0, 0)
    vec = lambda n: _resident((None, 1, n), lay)
    row = lambda n: pl.BlockSpec((tm, n), lambda i: (i, 0))
    args, in_specs = [x], [row(d)]
    if mix is not None:
        args += [mix[0], mix[1], w['ln_g'], w['ln_b'], w['w_pw2'], w['g_attn_out'], w['g_conv_out'], w['w_o']]
        in_specs += [row(512), row(512), vec(512), vec(512), _resident((None, 512, 512), lay),
                     vec(512), vec(512), _resident((None, d, d), lay)]
    args += [p, w['g_ffn'], w['w_up'], w['w_down'], w['g_ple'], w['w_ple_gate'], w['w_ple']]
    in_specs += [pl.BlockSpec((None, tm, p.shape[-1]), lambda i: (layer, i, 0)), vec(d),
                 _resident((None,) + w['w_up'].shape[1:], lay), _resident((None,) + w['w_down'].shape[1:], lay),
                 vec(d), _resident((None, d, d), lay), _resident((None,) + w['w_ple'].shape[1:], lay)]
    if g_final is not None:
        args.append(g_final)
        in_specs.append(_resident((1, d), lambda i: (0, 0)))
    return pl.pallas_call(
        functools.partial(_ffn_ple_kernel, mix is not None, g_final is not None),
        grid=(t // tm,),
        in_specs=in_specs, out_specs=row(d), out_shape=jax.ShapeDtypeStruct((t, d), F32),
        compiler_params=pltpu.CompilerParams(
            dimension_semantics=("parallel",), vmem_limit_bytes=VMEM_LIMIT),
        name=f"ffn_ple_l{layer}_{'s' if mix is not None else 'p'}",
    )(*args)


def _sample_in_proj_kernel(x_ref, cos_ref, sin_ref, g_mix_ref, w_in_ref, q_ref, k_ref, v_ref, u_ref):
    qs, k, v, u = _in_proj(x_ref[...], g_mix_ref[...], w_in_ref, cos_ref[...], sin_ref[...])
    for c, qc in enumerate(qs):
        q_ref[:, LANES * c:LANES * (c + 1)] = qc
    k_ref[...] = k
    v_ref[...] = v
    u_ref[...] = u


def _sample_in_proj(layer, x, cos, sin, w):
    t, d = x.shape
    lay = lambda *_: (layer, 0, 0)
    full = lambda n: pl.BlockSpec((t, n), lambda i: (0, 0))
    return pl.pallas_call(
        _sample_in_proj_kernel,
        grid=(1,),
        in_specs=[full(d), full(LANES), full(LANES), _resident((None, 1, d), lay),
                  _resident((None,) + w['w_in'].shape[1:], lay)],
        out_specs=[full(D_ATTN), full(D_KV), full(D_KV), full(512)],
        out_shape=[jax.ShapeDtypeStruct((t, n), F32) for n in (D_ATTN, D_KV, D_KV, 512)],
        compiler_params=pltpu.CompilerParams(
            dimension_semantics=("arbitrary",), vmem_limit_bytes=VMEM_LIMIT),
        name=f"sample_in_proj_l{layer}",
    )(x, cos, sin, w['g_mix'], w['w_in'])


def _sample_mixer_kernel(layer, n_tok,
                         sinks_ref, q_ref, kn_ref, vn_ref, un_ref, ck_ref, cv_ref, st_ref, w_dw_ref, b_dw_ref,
                         a_ref, cpre_ref, kw_ref, vw_ref, cs_ref, ue_scr):
    n_seq = ck_ref.shape[0]
    rows = n_seq * n_tok
    shift = n_tok.bit_length() - 1
    kc = ck_ref[...].reshape(n_seq * WINDOW, D_KV)
    vc = cv_ref[...].reshape(n_seq * WINDOW, D_KV)
    r = lax.broadcasted_iota(jnp.int32, (rows, n_seq * WINDOW), 0)
    c = lax.broadcasted_iota(jnp.int32, (rows, n_seq * WINDOW), 1)
    mask_cache = ((c >> 7) == (r >> shift)) & ((c & (WINDOW - 1)) >= (r & (n_tok - 1)))
    r = lax.broadcasted_iota(jnp.int32, (rows, rows), 0)
    c = lax.broadcasted_iota(jnp.int32, (rows, rows), 1)
    mask_new = ((c >> shift) == (r >> shift)) & ((c & (n_tok - 1)) <= (r & (n_tok - 1)))
    q_blocks = [q_ref[:, LANES * i:LANES * (i + 1)] for i in range(D_ATTN // LANES)]
    segments = [(kc.astype(BF16), vc.astype(BF16), mask_cache),
                (kn_ref[...].astype(BF16), vn_ref[...].astype(BF16), mask_new)]
    a_blocks = _attention(q_blocks, segments, lambda h: sinks_ref[layer, h])
    for i, ab in enumerate(a_blocks):
        a_ref[:, LANES * i:LANES * (i + 1)] = ab

    for s in range(n_seq):
        new = slice(n_tok * s, n_tok * (s + 1))
        kw_ref[s, 0:WINDOW - n_tok, :] = ck_ref[s, n_tok:WINDOW, :]
        kw_ref[s, WINDOW - n_tok:WINDOW, :] = kn_ref[new, :]
        vw_ref[s, 0:WINDOW - n_tok, :] = cv_ref[s, n_tok:WINDOW, :]
        vw_ref[s, WINDOW - n_tok:WINDOW, :] = vn_ref[new, :]
        ue_scr[s, 0:CONV_HIST, :] = st_ref[s]
        ue_scr[s, CONV_HIST:CONV_HIST + n_tok, :] = un_ref[new, :]
        cs_ref[s] = ue_scr[s, n_tok:n_tok + CONV_HIST, :]
        acc = jnp.zeros((n_tok, 512), F32)
        for j in range(CONV_K):
            acc = acc + ue_scr[s, j:j + n_tok, :] * w_dw_ref[j:j + 1, :]
        cpre_ref[new, :] = acc + b_dw_ref[...]


def _sample_mixer(layer, n_tok, q, kn, vn, un, cache_k, cache_v, state, w):
    n_b = cache_k.shape[1]
    g = SEQS_PER_STEP
    rows = g * n_tok
    lay = lambda *_: (layer, 0, 0)
    row = lambda n: pl.BlockSpec((rows, n), lambda i: (i, 0))
    seq = lambda m, n: pl.BlockSpec((None, g, m, n), lambda i: (layer, i, 0, 0))
    oseq = lambda m, n: pl.BlockSpec((g, m, n), lambda i: (i, 0, 0))
    return pl.pallas_call(
        functools.partial(_sample_mixer_kernel, layer, n_tok),
        grid=(n_b // g,),
        in_specs=[pl.BlockSpec(memory_space=pltpu.SMEM), row(D_ATTN), row(D_KV), row(D_KV), row(512),
                  seq(WINDOW, D_KV), seq(WINDOW, D_KV), seq(CONV_HIST, 512),
                  _resident((None, CONV_K, 512), lay), _resident((None, 1, 512), lay)],
        out_specs=[row(D_ATTN), row(512), oseq(WINDOW, D_KV), oseq(WINDOW, D_KV), oseq(CONV_HIST, 512)],
        out_shape=[jax.ShapeDtypeStruct((n_b * n_tok, D_ATTN), F32),
                   jax.ShapeDtypeStruct((n_b * n_tok, 512), F32),
                   jax.ShapeDtypeStruct((n_b, WINDOW, D_KV), F32),
                   jax.ShapeDtypeStruct((n_b, WINDOW, D_KV), F32),
                   jax.ShapeDtypeStruct((n_b, CONV_HIST, 512), F32)],
        scratch_shapes=[pltpu.VMEM((g, CONV_HIST + CONV_PAD // 2, 512), F32)],
        compiler_params=pltpu.CompilerParams(
            dimension_semantics=("parallel",), vmem_limit_bytes=VMEM_LIMIT),
        name=f"sample_mixer_l{layer}",
    )(w['sinks'], q, kn, vn, un, cache_k, cache_v, state, w['w_dw'], w['b_dw'])


def _rope_tables(pos):
    half = HEAD_DIM // 2
    inv = 1.0 / (ROPE_THETA ** (jnp.arange(half, dtype=F32) / half))
    ang = pos.astype(F32)[:, None] * inv[None, :]
    cos, sin = jnp.cos(ang), jnp.sin(ang)
    return jnp.tile(cos, (1, 4)), jnp.tile(jnp.concatenate([-sin, sin], axis=1), (1, 2))


def kernel(x_prompt, x_sample, p_prompt, p_sample, cache_k, cache_v, state_conv, g_mix, w_in, sinks, w_dw, b_dw,
           ln_g, ln_b, w_pw2, g_attn_out, g_conv_out, w_o, g_ffn, w_up, w_down, g_ple, w_ple_gate, w_ple, g_final):
    depth = w_in.shape[0]
    bp, sp, d = x_prompt.shape
    bs, ts, _ = x_sample.shape
    vec = lambda a: a.reshape(depth, 1, a.shape[-1])
    w = dict(
        sinks=sinks, g_mix=vec(g_mix), w_in=w_in.astype(BF16), w_dw=w_dw, b_dw=vec(b_dw), ln_g=vec(ln_g),
        ln_b=vec(ln_b), w_pw2=w_pw2.astype(BF16), g_attn_out=vec(g_attn_out), g_conv_out=vec(g_conv_out),
        w_o=w_o.astype(BF16), g_ffn=vec(g_ffn), w_up=w_up.astype(BF16), w_down=w_down.astype(BF16),
        g_ple=vec(g_ple), w_ple_gate=w_ple_gate.astype(BF16), w_ple=w_ple.astype(BF16))
    g_fin = g_final.reshape(1, d)

    cos_p, sin_p = _rope_tables(jnp.arange(sp, dtype=jnp.int32))
    pos_s = PAST_LEN + (jnp.arange(bs * ts, dtype=jnp.int32) % ts)
    cos_s, sin_s = _rope_tables(pos_s)

    pp = p_prompt.reshape(depth, bp * sp, -1)
    ps = p_sample.reshape(depth, bs * ts, -1)
    ck = cache_k.reshape(depth, bs, WINDOW, D_KV)
    cv = cache_v.reshape(depth, bs, WINDOW, D_KV)

    yp = x_prompt
    ys = x_sample.reshape(bs * ts, d)
    tm = 512
    kp, vp, up, ksl, vsl, usl = [], [], [], [], [], []
    for i in range(depth):
        last = g_fin if i == depth - 1 else None
        yp, kt, vt, ut = _prompt_mixer(i, yp, cos_p, sin_p, w, tm)
        yp = _ffn_ple(i, yp.reshape(bp * sp, d), pp, w, tm, g_final=last).reshape(bp, sp, d)
        kp.append(kt); vp.append(vt); up.append(ut[:, CONV_PAD - CONV_HIST:, :])

        q, kn, vn, un = _sample_in_proj(i, ys, cos_s, sin_s, w)
        a, cpre, kw, vw, cs = _sample_mixer(i, ts, q, kn, vn, un, ck, cv, state_conv, w)
        ys = _ffn_ple(i, ys, ps, w, bs * ts, mix=(a, cpre), g_final=last)
        ksl.append(kw); vsl.append(vw); usl.append(cs)

    heads = lambda lst, b: jnp.stack(lst).reshape(depth, b, WINDOW, N_KV_HEADS, HEAD_DIM)
    return (yp, ys.reshape(bs, ts, d), heads(kp, bp), heads(vp, bp), jnp.stack(up),
            heads(ksl, bs), heads(vsl, bs), jnp.stack(usl))
```

```python
import functools

import jax
import jax.numpy as jnp
from jax import lax
from jax.experimental import pallas as pl
from jax.experimental.pallas import tpu as pltpu

F32 = jnp.float32
BF16 = jnp.bfloat16

HEAD_DIM = 64
N_Q_HEADS = 8
N_KV_HEADS = 2
Q_PER_KV = N_Q_HEADS // N_KV_HEADS
D_ATTN = N_Q_HEADS * HEAD_DIM
D_KV = N_KV_HEADS * HEAD_DIM
WINDOW = 128
CONV_K = 31
CONV_HIST = CONV_K - 1
PAST_LEN = 8192
ROPE_THETA = 10000.0
EPS = 1e-6
NEG = -1e30

LANES = 128
Q_BLOCK = 128
CONV_PAD = 32
CONV_ROWS = 64
FF_CHUNK = 512
SEQS_PER_STEP = 8
VMEM_LIMIT = 56 * 1024 * 1024


def _rmsnorm(x, g):
    ms = jnp.mean(x * x, axis=-1, keepdims=True)
    return x * lax.rsqrt(ms + EPS) * g


def _dot(a, b):
    return jnp.dot(a, b, preferred_element_type=F32)


def _swap_half_heads(x):
    lane = lax.broadcasted_iota(jnp.int32, x.shape, 1)
    first = (lane & 32) == 0
    return jnp.where(first, pltpu.roll(x, LANES - 32, 1), pltpu.roll(x, 32, 1))


def _rope(x, cos, sin):
    return x * cos + _swap_half_heads(x) * sin


def _in_proj(x, g_mix, w_in_ref, cos, sin):
    h = _rmsnorm(x, g_mix).astype(BF16)
    q = _dot(h, w_in_ref[:, 0:D_ATTN])
    kv = _dot(h, w_in_ref[:, D_ATTN:D_ATTN + 2 * D_KV])
    o = D_ATTN + 2 * D_KV
    ua = _dot(h, w_in_ref[:, o:o + 512])
    ug = _dot(h, w_in_ref[:, o + 512:o + 1024])
    scale = HEAD_DIM ** -0.5
    qs = [_rope(q[:, LANES * c:LANES * (c + 1)], cos, sin) * scale for c in range(D_ATTN // LANES)]
    k = _rope(kv[:, 0:D_KV], cos, sin)
    v = kv[:, D_KV:2 * D_KV]
    u = ua * jax.nn.sigmoid(ug)
    return qs, k, v, u


def _attention(q_blocks, segments, sink_of_head):
    rows = q_blocks[0].shape[0]
    lane = lax.broadcasted_iota(jnp.int32, (rows, LANES), 1)
    low = lane < HEAD_DIM
    q_hat = []
    for h in range(N_Q_HEADS):
        g = h // Q_PER_KV
        blk = q_blocks[h // 2]
        if (h % 2) != g:
            blk = pltpu.roll(blk, HEAD_DIM, 1)
        q_hat.append(jnp.where(low if g == 0 else ~low, blk, 0.0).astype(BF16))
    q_hat = jnp.concatenate(q_hat, axis=0)
    scores = [lax.dot_general(q_hat, k, (((1,), (1,)), ((), ())), preferred_element_type=F32)
              for k, _, _ in segments]
    probs = [[] for _ in segments]
    dens = []
    for h in range(N_Q_HEADS):
        sink = sink_of_head(h)
        masked = [jnp.where(m, s[h * rows:(h + 1) * rows], NEG) for s, (_, _, m) in zip(scores, segments)]
        mx = sink
        for sm in masked:
            mx = jnp.maximum(mx, jnp.max(sm, axis=-1, keepdims=True))
        den = jnp.exp(sink - mx)
        for i, sm in enumerate(masked):
            e = jnp.exp(sm - mx)
            den = den + jnp.sum(e, axis=-1, keepdims=True)
            probs[i].append(e.astype(BF16))
        dens.append(den)
    out = None
    for i, (_, v, _) in enumerate(segments):
        o = _dot(jnp.concatenate(probs[i], axis=0), v)
        out = o if out is None else out + o
    placed = []
    for h in range(N_Q_HEADS):
        g = h // Q_PER_KV
        o = out[h * rows:(h + 1) * rows] / dens[h]
        if (h % 2) != g:
            o = pltpu.roll(o, HEAD_DIM, 1)
        placed.append(o)
    return [jnp.where(low, placed[2 * c], placed[2 * c + 1]) for c in range(D_ATTN // LANES)]


def _post_mix(a, cpre, ln_g, ln_b, w_pw2_ref, g_ao, g_co, w_o_ref):
    mu = jnp.mean(cpre, axis=-1, keepdims=True)
    xc = cpre - mu
    var = jnp.mean(xc * xc, axis=-1, keepdims=True)
    y = xc * lax.rsqrt(var + EPS) * ln_g + ln_b
    y = y * jax.nn.sigmoid(y)
    c = _dot(y.astype(BF16), w_pw2_ref[...])
    an = _rmsnorm(a, g_ao).astype(BF16)
    cn = _rmsnorm(c, g_co).astype(BF16)
    return _dot(an, w_o_ref[0:D_ATTN, :]) + _dot(cn, w_o_ref[D_ATTN:, :])


def _prompt_mixer_kernel(layer, tm,
                         sinks_ref, x_ref, cos_ref, sin_ref, g_mix_ref, w_in_ref, w_dw_ref, b_dw_ref,
                         ln_g_ref, ln_b_ref, w_pw2_ref, g_ao_ref, g_co_ref, w_o_ref,
                         x_out_ref, k_tail_ref, v_tail_ref, u_tail_ref,
                         q_scr, kt_scr, vs_scr, kt_hist, vs_hist, u_scr, a_scr, c_scr):
    s_idx = pl.program_id(1)
    slot = s_idx & 1
    lane = lax.broadcasted_iota(jnp.int32, (1, LANES), 1)
    half = [lane < HEAD_DIM, lane >= HEAD_DIM]
    ones_half = [jnp.where(half[p], 1.0, 0.0).astype(BF16) for p in range(2)]

    @pl.when(s_idx == 0)
    def _():
        for g in range(N_KV_HEADS):
            for p in range(2):
                kt_hist[0, g, p] = jnp.zeros((WINDOW, LANES), BF16)
                vs_hist[0, g, p, :, 0:LANES] = jnp.zeros((WINDOW, LANES), BF16)
                vs_scr[g, p, :, LANES:] = jnp.broadcast_to(ones_half[p], (tm, LANES))
                for sl in range(2):
                    vs_hist[sl, g, p, :, LANES:] = jnp.broadcast_to(ones_half[p], (WINDOW, LANES))
        for c in range(512 // LANES):
            u_scr[c, pl.ds(0, CONV_PAD, stride=2), :] = jnp.zeros((CONV_PAD, LANES), F32)

    x = x_ref[0]
    qs, k, v, u = _in_proj(x, g_mix_ref[...], w_in_ref, cos_ref[...], sin_ref[...])
    for c, qc in enumerate(qs):
        q_scr[:, LANES * c:LANES * (c + 1)] = qc.astype(BF16)
    k_swapped, v_swapped = pltpu.roll(k, HEAD_DIM, 1), pltpu.roll(v, HEAD_DIM, 1)
    for g in range(N_KV_HEADS):
        for p in range(2):
            kt_scr[g, p] = jnp.where(half[p], k if p == g else k_swapped, 0.0).astype(BF16)
            vs_scr[g, p, :, 0:LANES] = jnp.where(half[p], v if p == g else v_swapped, 0.0).astype(BF16)
    for c in range(512 // LANES):
        u_scr[c, pl.ds(2 * CONV_PAD, tm, stride=2), :] = u[:, LANES * c:LANES * (c + 1)]
    k_tail_ref[0] = k[tm - WINDOW:, :]
    v_tail_ref[0] = v[tm - WINDOW:, :]
    u_tail_ref[0] = u[tm - CONV_PAD:, :]

    r = lax.broadcasted_iota(jnp.int32, (Q_BLOCK, Q_BLOCK + WINDOW), 0)
    l = lax.broadcasted_iota(jnp.int32, (Q_BLOCK, Q_BLOCK + WINDOW), 1)
    band = (l >= r) & (l <= r + WINDOW)
    first_band = band & (l + (s_idx * tm - WINDOW) >= 0)
    nt = (((1,), (1,)), ((), ()))
    for j in range(tm // Q_BLOCK):
        rows = slice(Q_BLOCK * j, Q_BLOCK * (j + 1))

        def keys_of(cur, hist):
            if j == 0:
                return jnp.concatenate([hist[slot, g, p], cur[g, p, 0:Q_BLOCK]], axis=0)
            return cur[g, p, Q_BLOCK * j - WINDOW:Q_BLOCK * (j + 1)]

        mask = first_band if j == 0 else band
        for g in range(N_KV_HEADS):
            q_pair = jnp.concatenate([q_scr[rows, LANES * c:LANES * (c + 1)] for c in (2 * g, 2 * g + 1)], axis=0)
            probs, sink_terms = [], {}
            for p in range(2):
                s = lax.dot_general(q_pair, keys_of(kt_scr, kt_hist), nt, preferred_element_type=F32)
                es = []
                for i in range(2):
                    sink = sinks_ref[layer, 4 * g + 2 * i + p]
                    sm = jnp.where(mask, s[Q_BLOCK * i:Q_BLOCK * (i + 1)], NEG)
                    mx = jnp.maximum(jnp.max(sm, axis=-1, keepdims=True), sink)
                    es.append(jnp.exp(sm - mx).astype(BF16))
                    sink_terms[i, p] = jnp.exp(sink - mx)
                probs.append(jnp.concatenate(es, axis=0))
            acc = None
            for p in range(2):
                part = _dot(probs[p], keys_of(vs_scr, vs_hist))
                acc = part if acc is None else acc + part
            for i in range(2):
                c = 2 * g + i
                num = acc[Q_BLOCK * i:Q_BLOCK * (i + 1), 0:LANES]
                den = acc[Q_BLOCK * i:Q_BLOCK * (i + 1), LANES:] + jnp.where(half[0], sink_terms[i, 0], sink_terms[i, 1])
                a_scr[rows, LANES * c:LANES * (c + 1)] = num / den

    for r0 in range(0, tm, CONV_ROWS):
        for c in range(512 // LANES):
            cols = slice(LANES * c, LANES * (c + 1))
            acc = jnp.zeros((CONV_ROWS, LANES), F32)
            for j in range(CONV_K):
                lo = r0 + (CONV_PAD - CONV_HIST) + j
                acc = acc + u_scr[c, pl.ds(2 * lo, CONV_ROWS, stride=2), :] * w_dw_ref[j:j + 1, cols]
            c_scr[r0:r0 + CONV_ROWS, cols] = acc + b_dw_ref[:, cols]

    mix = _post_mix(a_scr[...], c_scr[...], ln_g_ref[...], ln_b_ref[...], w_pw2_ref,
                    g_ao_ref[...], g_co_ref[...], w_o_ref)
    x_out_ref[0] = x + mix

    for g in range(N_KV_HEADS):
        for p in range(2):
            kt_hist[1 - slot, g, p] = kt_scr[g, p, tm - WINDOW:tm, :]
            vs_hist[1 - slot, g, p, :, 0:LANES] = vs_scr[g, p, tm - WINDOW:tm, 0:LANES]
    for c in range(512 // LANES):
        u_scr[c, pl.ds(0, CONV_PAD, stride=2), :] = u_scr[c, pl.ds(2 * tm, CONV_PAD, stride=2), :]


def _resident(shape, index_map):
    return pl.BlockSpec(shape, index_map, pipeline_mode=pl.Buffered(1))


def _prompt_mixer(layer, x, cos, sin, w, tm):
    b, s, d = x.shape
    n_s = s // tm
    lay = lambda *_: (layer, 0, 0)
    vec = lambda n: _resident((None, 1, n), lay)
    in_specs = [
        pl.BlockSpec(memory_space=pltpu.SMEM),
        pl.BlockSpec((1, tm, d), lambda bi, si: (bi, si, 0)),
        pl.BlockSpec((tm, LANES), lambda bi, si: (si, 0)),
        pl.BlockSpec((tm, LANES), lambda bi, si: (si, 0)),
        vec(d),
        _resident((None,) + w['w_in'].shape[1:], lay),
        _resident((None, CONV_K, 512), lay),
        vec(512), vec(512), vec(512),
        _resident((None, 512, 512), lay),
        vec(512), vec(512),
        _resident((None, d, d), lay),
    ]
    out_shape = [
        jax.ShapeDtypeStruct((b, s, d), F32),
        jax.ShapeDtypeStruct((b, WINDOW, D_KV), F32),
        jax.ShapeDtypeStruct((b, WINDOW, D_KV), F32),
        jax.ShapeDtypeStruct((b, CONV_PAD, 512), F32),
    ]
    out_specs = [
        pl.BlockSpec((1, tm, d), lambda bi, si: (bi, si, 0)),
        pl.BlockSpec((1, WINDOW, D_KV), lambda bi, si: (bi, 0, 0)),
        pl.BlockSpec((1, WINDOW, D_KV), lambda bi, si: (bi, 0, 0)),
        pl.BlockSpec((1, CONV_PAD, 512), lambda bi, si: (bi, 0, 0)),
    ]
    scratch = [
        pltpu.VMEM((tm, D_ATTN), BF16),
        pltpu.VMEM((N_KV_HEADS, 2, tm, LANES), BF16),
        pltpu.VMEM((N_KV_HEADS, 2, tm, 2 * LANES), BF16),
        pltpu.VMEM((2, N_KV_HEADS, 2, WINDOW, LANES), BF16),
        pltpu.VMEM((2, N_KV_HEADS, 2, WINDOW, 2 * LANES), BF16),
        pltpu.VMEM((512 // LANES, 2 * (tm + CONV_PAD), LANES), F32),
        pltpu.VMEM((tm, D_ATTN), F32),
        pltpu.VMEM((tm, 512), F32),
    ]
    return pl.pallas_call(
        functools.partial(_prompt_mixer_kernel, layer, tm),
        grid=(b, n_s),
        in_specs=in_specs, out_specs=out_specs, out_shape=out_shape, scratch_shapes=scratch,
        compiler_params=pltpu.CompilerParams(
            dimension_semantics=("parallel", "arbitrary"), vmem_limit_bytes=VMEM_LIMIT),
        name=f"prompt_mixer_l{layer}",
    )(w['sinks'], x, cos, sin, w['g_mix'], w['w_in'], w['w_dw'], w['b_dw'], w['ln_g'], w['ln_b'],
      w['w_pw2'], w['g_attn_out'], w['g_conv_out'], w['w_o'])


def _ffn_ple_kernel(with_mix, final, *refs):
    refs = list(refs)
    x_ref = refs.pop(0)
    if with_mix:
        a_ref, cpre_ref, ln_g_ref, ln_b_ref, w_pw2_ref, g_ao_ref, g_co_ref, w_o_ref = refs[:8]
        refs = refs[8:]
    p_ref, g_ffn_ref, w_up_ref, w_down_ref, g_ple_ref, w_gate_ref, w_ple_ref = refs[:7]
    refs = refs[7:]
    if final:
        g_final_ref = refs.pop(0)
    (out_ref,) = refs

    x = x_ref[...]
    if with_mix:
        x = x + _post_mix(a_ref[...], cpre_ref[...], ln_g_ref[...], ln_b_ref[...], w_pw2_ref,
                          g_ao_ref[...], g_co_ref[...], w_o_ref)
    xn = _rmsnorm(x, g_ffn_ref[...]).astype(BF16)
    d_ff = w_up_ref.shape[1]
    acc = None
    for c in range(d_ff // FF_CHUNK):
        hc = _dot(xn, w_up_ref[:, FF_CHUNK * c:FF_CHUNK * (c + 1)])
        hc = jnp.square(jnp.maximum(hc, 0.0)).astype(BF16)
        part = _dot(hc, w_down_ref[FF_CHUNK * c:FF_CHUNK * (c + 1), :])
        acc = part if acc is None else acc + part
    x = x + acc
    gate = jax.nn.sigmoid(_dot(_rmsnorm(x, g_ple_ref[...]).astype(BF16), w_gate_ref[...]))
    x = x + _dot(p_ref[...].astype(BF16), w_ple_ref[...]) * gate
    if final:
        x = _rmsnorm(x, g_final_ref[...])
    out_ref[...] = x


def _ffn_ple(layer, x, p, w, tm, mix=None, g_final=None):
    t, d = x.shape
    lay = lambda *_: (layer, 0, 0)
    vec = lambda n: _resident((None, 1, n), lay)
    row = lambda n: pl.BlockSpec((tm, n), lambda i: (i, 0))
    args, in_specs = [x], [row(d)]
    if mix is not None:
        args += [mix[0], mix[1], w['ln_g'], w['ln_b'], w['w_pw2'], w['g_attn_out'], w['g_conv_out'], w['w_o']]
        in_specs += [row(512), row(512), vec(512), vec(512), _resident((None, 512, 512), lay),
                     vec(512), vec(512), _resident((None, d, d), lay)]
    args += [p, w['g_ffn'], w['w_up'], w['w_down'], w['g_ple'], w['w_ple_gate'], w['w_ple']]
    in_specs += [pl.BlockSpec((None, tm, p.shape[-1]), lambda i: (layer, i, 0)), vec(d),
                 _resident((None,) + w['w_up'].shape[1:], lay), _resident((None,) + w['w_down'].shape[1:], lay),
                 vec(d), _resident((None, d, d), lay), _resident((None,) + w['w_ple'].shape[1:], lay)]
    if g_final is not None:
        args.append(g_final)
        in_specs.append(_resident((1, d), lambda i: (0, 0)))
    return pl.pallas_call(
        functools.partial(_ffn_ple_kernel, mix is not None, g_final is not None),
        grid=(t // tm,),
        in_specs=in_specs, out_specs=row(d), out_shape=jax.ShapeDtypeStruct((t, d), F32),
        compiler_params=pltpu.CompilerParams(
            dimension_semantics=("parallel",), vmem_limit_bytes=VMEM_LIMIT),
        name=f"ffn_ple_l{layer}_{'s' if mix is not None else 'p'}",
    )(*args)


def _sample_in_proj_kernel(x_ref, cos_ref, sin_ref, g_mix_ref, w_in_ref, q_ref, k_ref, v_ref, u_ref):
    qs, k, v, u = _in_proj(x_ref[...], g_mix_ref[...], w_in_ref, cos_ref[...], sin_ref[...])
    for c, qc in enumerate(qs):
        q_ref[:, LANES * c:LANES * (c + 1)] = qc
    k_ref[...] = k
    v_ref[...] = v
    u_ref[...] = u


def _sample_in_proj(layer, x, cos, sin, w):
    t, d = x.shape
    lay = lambda *_: (layer, 0, 0)
    full = lambda n: pl.BlockSpec((t, n), lambda i: (0, 0))
    return pl.pallas_call(
        _sample_in_proj_kernel,
        grid=(1,),
        in_specs=[full(d), full(LANES), full(LANES), _resident((None, 1, d), lay),
                  _resident((None,) + w['w_in'].shape[1:], lay)],
        out_specs=[full(D_ATTN), full(D_KV), full(D_KV), full(512)],
        out_shape=[jax.ShapeDtypeStruct((t, n), F32) for n in (D_ATTN, D_KV, D_KV, 512)],
        compiler_params=pltpu.CompilerParams(
            dimension_semantics=("arbitrary",), vmem_limit_bytes=VMEM_LIMIT),
        name=f"sample_in_proj_l{layer}",
    )(x, cos, sin, w['g_mix'], w['w_in'])


def _sample_mixer_kernel(layer, n_tok,
                         sinks_ref, q_ref, kn_ref, vn_ref, un_ref, ck_ref, cv_ref, st_ref, w_dw_ref, b_dw_ref,
                         a_ref, cpre_ref, kw_ref, vw_ref, cs_ref, ue_scr):
    n_seq = ck_ref.shape[0]
    rows = n_seq * n_tok
    shift = n_tok.bit_length() - 1
    kc = ck_ref[...].reshape(n_seq * WINDOW, D_KV)
    vc = cv_ref[...].reshape(n_seq * WINDOW, D_KV)
    r = lax.broadcasted_iota(jnp.int32, (rows, n_seq * WINDOW), 0)
    c = lax.broadcasted_iota(jnp.int32, (rows, n_seq * WINDOW), 1)
    mask_cache = ((c >> 7) == (r >> shift)) & ((c & (WINDOW - 1)) >= (r & (n_tok - 1)))
    r = lax.broadcasted_iota(jnp.int32, (rows, rows), 0)
    c = lax.broadcasted_iota(jnp.int32, (rows, rows), 1)
    mask_new = ((c >> shift) == (r >> shift)) & ((c & (n_tok - 1)) <= (r & (n_tok - 1)))
    q_blocks = [q_ref[:, LANES * i:LANES * (i + 1)] for i in range(D_ATTN // LANES)]
    segments = [(kc.astype(BF16), vc.astype(BF16), mask_cache),
                (kn_ref[...].astype(BF16), vn_ref[...].astype(BF16), mask_new)]
    a_blocks = _attention(q_blocks, segments, lambda h: sinks_ref[layer, h])
    for i, ab in enumerate(a_blocks):
        a_ref[:, LANES * i:LANES * (i + 1)] = ab

    for s in range(n_seq):
        new = slice(n_tok * s, n_tok * (s + 1))
        kw_ref[s, 0:WINDOW - n_tok, :] = ck_ref[s, n_tok:WINDOW, :]
        kw_ref[s, WINDOW - n_tok:WINDOW, :] = kn_ref[new, :]
        vw_ref[s, 0:WINDOW - n_tok, :] = cv_ref[s, n_tok:WINDOW, :]
        vw_ref[s, WINDOW - n_tok:WINDOW, :] = vn_ref[new, :]
        ue_scr[s, 0:CONV_HIST, :] = st_ref[s]
        ue_scr[s, CONV_HIST:CONV_HIST + n_tok, :] = un_ref[new, :]
        cs_ref[s] = ue_scr[s, n_tok:n_tok + CONV_HIST, :]
        acc = jnp.zeros((n_tok, 512), F32)
        for j in range(CONV_K):
            acc = acc + ue_scr[s, j:j + n_tok, :] * w_dw_ref[j:j + 1, :]
        cpre_ref[new, :] = acc + b_dw_ref[...]


def _sample_mixer(layer, n_tok, q, kn, vn, un, cache_k, cache_v, state, w):
    n_b = cache_k.shape[1]
    g = SEQS_PER_STEP
    rows = g * n_tok
    lay = lambda *_: (layer, 0, 0)
    row = lambda n: pl.BlockSpec((rows, n), lambda i: (i, 0))
    seq = lambda m, n: pl.BlockSpec((None, g, m, n), lambda i: (layer, i, 0, 0))
    oseq = lambda m, n: pl.BlockSpec((g, m, n), lambda i: (i, 0, 0))
    return pl.pallas_call(
        functools.partial(_sample_mixer_kernel, layer, n_tok),
        grid=(n_b // g,),
        in_specs=[pl.BlockSpec(memory_space=pltpu.SMEM), row(D_ATTN), row(D_KV), row(D_KV), row(512),
                  seq(WINDOW, D_KV), seq(WINDOW, D_KV), seq(CONV_HIST, 512),
                  _resident((None, CONV_K, 512), lay), _resident((None, 1, 512), lay)],
        out_specs=[row(D_ATTN), row(512), oseq(WINDOW, D_KV), oseq(WINDOW, D_KV), oseq(CONV_HIST, 512)],
        out_shape=[jax.ShapeDtypeStruct((n_b * n_tok, D_ATTN), F32),
                   jax.ShapeDtypeStruct((n_b * n_tok, 512), F32),
                   jax.ShapeDtypeStruct((n_b, WINDOW, D_KV), F32),
                   jax.ShapeDtypeStruct((n_b, WINDOW, D_KV), F32),
                   jax.ShapeDtypeStruct((n_b, CONV_HIST, 512), F32)],
        scratch_shapes=[pltpu.VMEM((g, CONV_HIST + CONV_PAD // 2, 512), F32)],
        compiler_params=pltpu.CompilerParams(
            dimension_semantics=("parallel",), vmem_limit_bytes=VMEM_LIMIT),
        name=f"sample_mixer_l{layer}",
    )(w['sinks'], q, kn, vn, un, cache_k, cache_v, state, w['w_dw'], w['b_dw'])


def _rope_tables(pos):
    half = HEAD_DIM // 2
    inv = 1.0 / (ROPE_THETA ** (jnp.arange(half, dtype=F32) / half))
    ang = pos.astype(F32)[:, None] * inv[None, :]
    cos, sin = jnp.cos(ang), jnp.sin(ang)
    return jnp.tile(cos, (1, 4)), jnp.tile(jnp.concatenate([-sin, sin], axis=1), (1, 2))


def kernel(x_prompt, x_sample, p_prompt, p_sample, cache_k, cache_v, state_conv, g_mix, w_in, sinks, w_dw, b_dw,
           ln_g, ln_b, w_pw2, g_attn_out, g_conv_out, w_o, g_ffn, w_up, w_down, g_ple, w_ple_gate, w_ple, g_final):
    depth = w_in.shape[0]
    bp, sp, d = x_prompt.shape
    bs, ts, _ = x_sample.shape
    vec = lambda a: a.reshape(depth, 1, a.shape[-1])
    w = dict(
        sinks=sinks, g_mix=vec(g_mix), w_in=w_in.astype(BF16), w_dw=w_dw, b_dw=vec(b_dw), ln_g=vec(ln_g),
        ln_b=vec(ln_b), w_pw2=w_pw2.astype(BF16), g_attn_out=vec(g_attn_out), g_conv_out=vec(g_conv_out),
        w_o=w_o.astype(BF16), g_ffn=vec(g_ffn), w_up=w_up.astype(BF16), w_down=w_down.astype(BF16),
        g_ple=vec(g_ple), w_ple_gate=w_ple_gate.astype(BF16), w_ple=w_ple.astype(BF16))
    g_fin = g_final.reshape(1, d)

    cos_p, sin_p = _rope_tables(jnp.arange(sp, dtype=jnp.int32))
    pos_s = PAST_LEN + (jnp.arange(bs * ts, dtype=jnp.int32) % ts)
    cos_s, sin_s = _rope_tables(pos_s)

    pp = p_prompt.reshape(depth, bp * sp, -1)
    ps = p_sample.reshape(depth, bs * ts, -1)
    ck = cache_k.reshape(depth, bs, WINDOW, D_KV)
    cv = cache_v.reshape(depth, bs, WINDOW, D_KV)

    yp = x_prompt
    ys = x_sample.reshape(bs * ts, d)
    tm = 512
    kp, vp, up, ksl, vsl, usl = [], [], [], [], [], []
    for i in range(depth):
        last = g_fin if i == depth - 1 else None
        yp, kt, vt, ut = _prompt_mixer(i, yp, cos_p, sin_p, w, tm)
        yp = _ffn_ple(i, yp.reshape(bp * sp, d), pp, w, tm, g_final=last).reshape(bp, sp, d)
        kp.append(kt); vp.append(vt); up.append(ut[:, CONV_PAD - CONV_HIST:, :])

        q, kn, vn, un = _sample_in_proj(i, ys, cos_s, sin_s, w)
        a, cpre, kw, vw, cs = _sample_mixer(i, ts, q, kn, vn, un, ck, cv, state_conv, w)
        ys = _ffn_ple(i, ys, ps, w, bs * ts, mix=(a, cpre), g_final=last)
        ksl.append(kw); vsl.append(vw); usl.append(cs)

    heads = lambda lst, b: jnp.stack(lst).reshape(depth, b, WINDOW, N_KV_HEADS, HEAD_DIM)
    return (yp, ys.reshape(bs, ts, d), heads(kp, bp), heads(vp, bp), jnp.stack(up),
            heads(ksl, bs), heads(vsl, bs), jnp.stack(usl))
```

```python
import functools

import jax
import jax.numpy as jnp
from jax import lax
from jax.experimental import pallas as pl
from jax.experimental.pallas import tpu as pltpu

F32 = jnp.float32
BF16 = jnp.bfloat16

HEAD_DIM = 64
N_Q_HEADS = 8
N_KV_HEADS = 2
Q_PER_KV = N_Q_HEADS // N_KV_HEADS
D_ATTN = N_Q_HEADS * HEAD_DIM
D_KV = N_KV_HEADS * HEAD_DIM
WINDOW = 128
CONV_K = 31
CONV_HIST = CONV_K - 1
PAST_LEN = 8192
ROPE_THETA = 10000.0
EPS = 1e-6
NEG = -1e30

LANES = 128
Q_BLOCK = 128
CONV_PAD = 32
CONV_ROWS = 64
FF_CHUNK = 512
SEQS_PER_STEP = 8
VMEM_LIMIT = 56 * 1024 * 1024


def _rmsnorm(x, g):
    ms = jnp.mean(x * x, axis=-1, keepdims=True)
    return x * lax.rsqrt(ms + EPS) * g


def _dot(a, b):
    return jnp.dot(a, b, preferred_element_type=F32)


def _swap_half_heads(x):
    lane = lax.broadcasted_iota(jnp.int32, x.shape, 1)
    first = (lane & 32) == 0
    return jnp.where(first, pltpu.roll(x, LANES - 32, 1), pltpu.roll(x, 32, 1))


def _rope(x, cos, sin):
    return x * cos + _swap_half_heads(x) * sin


def _in_proj(x, g_mix, w_in_ref, cos, sin):
    h = _rmsnorm(x, g_mix).astype(BF16)
    q = _dot(h, w_in_ref[:, 0:D_ATTN])
    kv = _dot(h, w_in_ref[:, D_ATTN:D_ATTN + 2 * D_KV])
    o = D_ATTN + 2 * D_KV
    ua = _dot(h, w_in_ref[:, o:o + 512])
    ug = _dot(h, w_in_ref[:, o + 512:o + 1024])
    scale = HEAD_DIM ** -0.5
    qs = [_rope(q[:, LANES * c:LANES * (c + 1)], cos, sin) * scale for c in range(D_ATTN // LANES)]
    k = _rope(kv[:, 0:D_KV], cos, sin)
    v = kv[:, D_KV:2 * D_KV]
    u = ua * jax.nn.sigmoid(ug)
    return qs, k, v, u


def _attention(q_blocks, segments, sink_of_head):
    rows = q_blocks[0].shape[0]
    lane = lax.broadcasted_iota(jnp.int32, (rows, LANES), 1)
    low = lane < HEAD_DIM
    q_hat = []
    for h in range(N_Q_HEADS):
        g = h // Q_PER_KV
        blk = q_blocks[h // 2]
        if (h % 2) != g:
            blk = pltpu.roll(blk, HEAD_DIM, 1)
        q_hat.append(jnp.where(low if g == 0 else ~low, blk, 0.0).astype(BF16))
    q_hat = jnp.concatenate(q_hat, axis=0)
    scores = [lax.dot_general(q_hat, k, (((1,), (1,)), ((), ())), preferred_element_type=F32)
              for k, _, _ in segments]
    probs = [[] for _ in segments]
    dens = []
    for h in range(N_Q_HEADS):
        sink = sink_of_head(h)
        masked = [jnp.where(m, s[h * rows:(h + 1) * rows], NEG) for s, (_, _, m) in zip(scores, segments)]
        mx = sink
        for sm in masked:
            mx = jnp.maximum(mx, jnp.max(sm, axis=-1, keepdims=True))
        den = jnp.exp(sink - mx)
        for i, sm in enumerate(masked):
            e = jnp.exp(sm - mx)
            den = den + jnp.sum(e, axis=-1, keepdims=True)
            probs[i].append(e.astype(BF16))
        dens.append(den)
    out = None
    for i, (_, v, _) in enumerate(segments):
        o = _dot(jnp.concatenate(probs[i], axis=0), v)
        out = o if out is None else out + o
    placed = []
    for h in range(N_Q_HEADS):
        g = h // Q_PER_KV
        o = out[h * rows:(h + 1) * rows] / dens[h]
        if (h % 2) != g:
            o = pltpu.roll(o, HEAD_DIM, 1)
        placed.append(o)
    return [jnp.where(low, placed[2 * c], placed[2 * c + 1]) for c in range(D_ATTN // LANES)]


def _post_mix(a, cpre, ln_g, ln_b, w_pw2_ref, g_ao, g_co, w_o_ref):
    mu = jnp.mean(cpre, axis=-1, keepdims=True)
    xc = cpre - mu
    var = jnp.mean(xc * xc, axis=-1, keepdims=True)
    y = xc * lax.rsqrt(var + EPS) * ln_g + ln_b
    y = y * jax.nn.sigmoid(y)
    c = _dot(y.astype(BF16), w_pw2_ref[...])
    an = _rmsnorm(a, g_ao).astype(BF16)
    cn = _rmsnorm(c, g_co).astype(BF16)
    return _dot(an, w_o_ref[0:D_ATTN, :]) + _dot(cn, w_o_ref[D_ATTN:, :])


def _resident(shape, index_map):
    return pl.BlockSpec(shape, index_map, pipeline_mode=pl.Buffered(1))


def _layer_kernel(layer, tm, n_s, n_tiles, final, *refs):
    (sinks_ref, x_ref, cos_ref, sin_ref, p_ref, xs_ref, ps_ref, g_mix_ref, w_in_ref, w_dw_ref, b_dw_ref,
     ln_g_ref, ln_b_ref, w_pw2_ref, g_ao_ref, g_co_ref, w_o_ref, g_ffn_ref, w_up_ref, w_down_ref, g_ple_ref,
     w_gate_ref, w_ple_ref) = refs[:23]
    refs = refs[23:]
    g_final = None
    if final:
        g_final = refs[0][...]
        refs = refs[1:]
    (y_ref, ys_ref, k_tail_ref, v_tail_ref, u_tail_ref,
     x1_scr, q_scr, kt_scr, vs_scr, kt_hist, vs_hist, u_scr, a_scr, c_scr) = refs

    t = pl.program_id(0)
    slot = t & 1
    s_idx = lax.rem(jnp.minimum(t, n_tiles - 1), n_s)
    lane = lax.broadcasted_iota(jnp.int32, (1, LANES), 1)
    half = [lane < HEAD_DIM, lane >= HEAD_DIM]
    ones_half = [jnp.where(half[p], 1.0, 0.0).astype(BF16) for p in range(2)]

    @pl.when(t == 0)
    def _():
        x1_scr[1] = xs_ref[...]
        for g in range(N_KV_HEADS):
            for p in range(2):
                vs_scr[g, p, :, LANES:] = jnp.broadcast_to(ones_half[p], (tm, LANES))
                for sl in range(2):
                    vs_hist[sl, g, p, :, LANES:] = jnp.broadcast_to(ones_half[p], (WINDOW, LANES))

    @pl.when(s_idx == 0)
    def _():
        for g in range(N_KV_HEADS):
            for p in range(2):
                kt_hist[slot, g, p] = jnp.zeros((WINDOW, LANES), BF16)
                vs_hist[slot, g, p, :, 0:LANES] = jnp.zeros((WINDOW, LANES), BF16)
        for c in range(512 // LANES):
            u_scr[c, pl.ds(0, CONV_PAD, stride=2), :] = jnp.zeros((CONV_PAD, LANES), F32)

    xf = x1_scr[1 - slot]
    xfn = _rmsnorm(xf, g_ffn_ref[...]).astype(BF16)
    ffn_acc = None

    x = x_ref[...]
    qs, k, v, u = _in_proj(x, g_mix_ref[...], w_in_ref, cos_ref[...], sin_ref[...])
    for c, qc in enumerate(qs):
        q_scr[:, LANES * c:LANES * (c + 1)] = qc.astype(BF16)
    k_swapped, v_swapped = pltpu.roll(k, HEAD_DIM, 1), pltpu.roll(v, HEAD_DIM, 1)
    for g in range(N_KV_HEADS):
        for p in range(2):
            kt_scr[g, p] = jnp.where(half[p], k if p == g else k_swapped, 0.0).astype(BF16)
            vs_scr[g, p, :, 0:LANES] = jnp.where(half[p], v if p == g else v_swapped, 0.0).astype(BF16)
    for c in range(512 // LANES):
        u_scr[c, pl.ds(2 * CONV_PAD, tm, stride=2), :] = u[:, LANES * c:LANES * (c + 1)]
    k_tail_ref[0] = k[tm - WINDOW:, :]
    v_tail_ref[0] = v[tm - WINDOW:, :]
    u_tail_ref[0] = u[tm - CONV_PAD:, :]

    r = lax.broadcasted_iota(jnp.int32, (Q_BLOCK, Q_BLOCK + WINDOW), 0)
    l = lax.broadcasted_iota(jnp.int32, (Q_BLOCK, Q_BLOCK + WINDOW), 1)
    band = (l >= r) & (l <= r + WINDOW)
    first_band = band & (l + (s_idx * tm - WINDOW) >= 0)
    nt = (((1,), (1,)), ((), ()))
    for unit in range(w_up_ref.shape[1] // FF_CHUNK):
        hc = _dot(xfn, w_up_ref[:, FF_CHUNK * unit:FF_CHUNK * (unit + 1)])
        hc = jnp.square(jnp.maximum(hc, 0.0)).astype(BF16)
        part = _dot(hc, w_down_ref[FF_CHUNK * unit:FF_CHUNK * (unit + 1), :])
        ffn_acc = part if ffn_acc is None else ffn_acc + part

        r0 = CONV_ROWS * unit
        for c in range(512 // LANES):
            cols = slice(LANES * c, LANES * (c + 1))
            conv = jnp.zeros((CONV_ROWS, LANES), F32)
            for tap in range(CONV_K):
                lo = r0 + (CONV_PAD - CONV_HIST) + tap
                conv = conv + u_scr[c, pl.ds(2 * lo, CONV_ROWS, stride=2), :] * w_dw_ref[tap:tap + 1, cols]
            c_scr[r0:r0 + CONV_ROWS, cols] = conv + b_dw_ref[:, cols]

        j, g = divmod(unit, N_KV_HEADS)
        rows = slice(Q_BLOCK * j, Q_BLOCK * (j + 1))
        mask = first_band if j == 0 else band

        def keys_of(cur, hist, p):
            if j == 0:
                return jnp.concatenate([hist[slot, g, p], cur[g, p, 0:Q_BLOCK]], axis=0)
            return cur[g, p, Q_BLOCK * j - WINDOW:Q_BLOCK * (j + 1)]

        q_pair = jnp.concatenate([q_scr[rows, LANES * c:LANES * (c + 1)] for c in (2 * g, 2 * g + 1)], axis=0)
        probs, sink_terms = [], {}
        for p in range(2):
            s = lax.dot_general(q_pair, keys_of(kt_scr, kt_hist, p), nt, preferred_element_type=F32)
            es = []
            for i in range(2):
                sink = sinks_ref[layer, 4 * g + 2 * i + p]
                sm = jnp.where(mask, s[Q_BLOCK * i:Q_BLOCK * (i + 1)], NEG)
                mx = jnp.maximum(jnp.max(sm, axis=-1, keepdims=True), sink)
                es.append(jnp.exp(sm - mx).astype(BF16))
                sink_terms[i, p] = jnp.exp(sink - mx)
            probs.append(jnp.concatenate(es, axis=0))
        acc = _dot(probs[0], keys_of(vs_scr, vs_hist, 0)) + _dot(probs[1], keys_of(vs_scr, vs_hist, 1))
        for i in range(2):
            c = 2 * g + i
            num = acc[Q_BLOCK * i:Q_BLOCK * (i + 1), 0:LANES]
            den = acc[Q_BLOCK * i:Q_BLOCK * (i + 1), LANES:] + jnp.where(half[0], sink_terms[i, 0], sink_terms[i, 1])
            a_scr[rows, LANES * c:LANES * (c + 1)] = num / den

    xf = xf + ffn_acc
    gate = jax.nn.sigmoid(_dot(_rmsnorm(xf, g_ple_ref[...]).astype(BF16), w_gate_ref[...]))
    p_rows = jnp.where(t == 0, ps_ref[...], p_ref[...])
    xf = xf + _dot(p_rows.astype(BF16), w_ple_ref[...]) * gate
    if g_final is not None:
        xf = _rmsnorm(xf, g_final)
    y_ref[...] = xf

    mix = _post_mix(a_scr[...], c_scr[...], ln_g_ref[...], ln_b_ref[...], w_pw2_ref,
                    g_ao_ref[...], g_co_ref[...], w_o_ref)
    x1_scr[slot] = x + mix

    for g in range(N_KV_HEADS):
        for p in range(2):
            kt_hist[1 - slot, g, p] = kt_scr[g, p, tm - WINDOW:tm, :]
            vs_hist[1 - slot, g, p, :, 0:LANES] = vs_scr[g, p, tm - WINDOW:tm, 0:LANES]
    for c in range(512 // LANES):
        u_scr[c, pl.ds(0, CONV_PAD, stride=2), :] = u_scr[c, pl.ds(2 * tm, CONV_PAD, stride=2), :]

    @pl.when(t == 0)
    def _():
        ys_ref[...] = y_ref[...]


def _layer(layer, x, cos, sin, p, xs, ps, w, g_final=None):
    b, s, d = x.shape
    tm = xs.shape[0]
    n_s = s // tm
    n_tiles = b * n_s
    assert w['w_up'].shape[-1] // FF_CHUNK == (tm // Q_BLOCK) * N_KV_HEADS == tm // CONV_ROWS
    lay = lambda *_: (layer, 0, 0)
    vec = lambda n: _resident((None, 1, n), lay)
    mixer_tile = lambda t: jnp.minimum(t, n_tiles - 1)
    ffn_tile = lambda t: jnp.maximum(t - 1, 0)
    seq_of = lambda t: mixer_tile(t) // n_s
    in_specs = [
        pl.BlockSpec(memory_space=pltpu.SMEM),
        pl.BlockSpec((tm, d), lambda t: (mixer_tile(t), 0)),
        pl.BlockSpec((tm, LANES), lambda t: (lax.rem(mixer_tile(t), n_s), 0)),
        pl.BlockSpec((tm, LANES), lambda t: (lax.rem(mixer_tile(t), n_s), 0)),
        pl.BlockSpec((None, tm, p.shape[-1]), lambda t: (layer, ffn_tile(t), 0)),
        _resident((tm, d), lambda t: (0, 0)),
        _resident((None, tm, p.shape[-1]), lay),
        vec(d),
        _resident((None,) + w['w_in'].shape[1:], lay),
        _resident((None, CONV_K, 512), lay),
        vec(512), vec(512), vec(512),
        _resident((None, 512, 512), lay),
        vec(512), vec(512),
        _resident((None, d, d), lay),
        vec(d),
        _resident((None,) + w['w_up'].shape[1:], lay), _resident((None,) + w['w_down'].shape[1:], lay),
        vec(d),
        _resident((None, d, d), lay), _resident((None,) + w['w_ple'].shape[1:], lay),
    ]
    args = [w['sinks'], x.reshape(b * s, d), cos, sin, p, xs, ps, w['g_mix'], w['w_in'], w['w_dw'], w['b_dw'],
            w['ln_g'], w['ln_b'], w['w_pw2'], w['g_attn_out'], w['g_conv_out'], w['w_o'], w['g_ffn'], w['w_up'],
            w['w_down'], w['g_ple'], w['w_ple_gate'], w['w_ple']]
    if g_final is not None:
        args.append(g_final)
        in_specs.append(_resident((1, d), lambda t: (0, 0)))
    out_shape = [
        jax.ShapeDtypeStruct((b * s, d), F32),
        jax.ShapeDtypeStruct((tm, d), F32),
        jax.ShapeDtypeStruct((b, WINDOW, D_KV), F32),
        jax.ShapeDtypeStruct((b, WINDOW, D_KV), F32),
        jax.ShapeDtypeStruct((b, CONV_PAD, 512), F32),
    ]
    out_specs = [
        pl.BlockSpec((tm, d), lambda t: (ffn_tile(t), 0)),
        pl.BlockSpec((tm, d), lambda t: (0, 0)),
        pl.BlockSpec((1, WINDOW, D_KV), lambda t: (seq_of(t), 0, 0)),
        pl.BlockSpec((1, WINDOW, D_KV), lambda t: (seq_of(t), 0, 0)),
        pl.BlockSpec((1, CONV_PAD, 512), lambda t: (seq_of(t), 0, 0)),
    ]
    scratch = [
        pltpu.VMEM((2, tm, d), F32),
        pltpu.VMEM((tm, D_ATTN), BF16),
        pltpu.VMEM((N_KV_HEADS, 2, tm, LANES), BF16),
        pltpu.VMEM((N_KV_HEADS, 2, tm, 2 * LANES), BF16),
        pltpu.VMEM((2, N_KV_HEADS, 2, WINDOW, LANES), BF16),
        pltpu.VMEM((2, N_KV_HEADS, 2, WINDOW, 2 * LANES), BF16),
        pltpu.VMEM((512 // LANES, 2 * (tm + CONV_PAD), LANES), F32),
        pltpu.VMEM((tm, D_ATTN), F32),
        pltpu.VMEM((tm, 512), F32),
    ]
    return pl.pallas_call(
        functools.partial(_layer_kernel, layer, tm, n_s, n_tiles, g_final is not None),
        grid=(n_tiles + 1,),
        in_specs=in_specs, out_specs=out_specs, out_shape=out_shape, scratch_shapes=scratch,
        compiler_params=pltpu.CompilerParams(
            dimension_semantics=("arbitrary",), vmem_limit_bytes=VMEM_LIMIT),
        name=f"layer_l{layer}",
    )(*args)


def _sample_out_proj_kernel(x_ref, a_ref, cpre_ref, ln_g_ref, ln_b_ref, w_pw2_ref, g_ao_ref, g_co_ref, w_o_ref,
                            out_ref):
    out_ref[...] = x_ref[...] + _post_mix(a_ref[...], cpre_ref[...], ln_g_ref[...], ln_b_ref[...], w_pw2_ref,
                                          g_ao_ref[...], g_co_ref[...], w_o_ref)


def _sample_out_proj(layer, x, a, cpre, w):
    t, d = x.shape
    lay = lambda *_: (layer, 0, 0)
    vec = lambda n: _resident((None, 1, n), lay)
    row = lambda n: pl.BlockSpec((t, n), lambda i: (0, 0))
    return pl.pallas_call(
        _sample_out_proj_kernel,
        grid=(1,),
        in_specs=[row(d), row(512), row(512), vec(512), vec(512), _resident((None, 512, 512), lay),
                  vec(512), vec(512), _resident((None, d, d), lay)],
        out_specs=row(d), out_shape=jax.ShapeDtypeStruct((t, d), F32),
        compiler_params=pltpu.CompilerParams(
            dimension_semantics=("arbitrary",), vmem_limit_bytes=VMEM_LIMIT),
        name=f"sample_out_proj_l{layer}",
    )(x, a, cpre, w['ln_g'], w['ln_b'], w['w_pw2'], w['g_attn_out'], w['g_conv_out'], w['w_o'])


def _sample_in_proj_kernel(x_ref, cos_ref, sin_ref, g_mix_ref, w_in_ref, q_ref, k_ref, v_ref, u_ref):
    qs, k, v, u = _in_proj(x_ref[...], g_mix_ref[...], w_in_ref, cos_ref[...], sin_ref[...])
    for c, qc in enumerate(qs):
        q_ref[:, LANES * c:LANES * (c + 1)] = qc
    k_ref[...] = k
    v_ref[...] = v
    u_ref[...] = u


def _sample_in_proj(layer, x, cos, sin, w):
    t, d = x.shape
    lay = lambda *_: (layer, 0, 0)
    full = lambda n: pl.BlockSpec((t, n), lambda i: (0, 0))
    return pl.pallas_call(
        _sample_in_proj_kernel,
        grid=(1,),
        in_specs=[full(d), full(LANES), full(LANES), _resident((None, 1, d), lay),
                  _resident((None,) + w['w_in'].shape[1:], lay)],
        out_specs=[full(D_ATTN), full(D_KV), full(D_KV), full(512)],
        out_shape=[jax.ShapeDtypeStruct((t, n), F32) for n in (D_ATTN, D_KV, D_KV, 512)],
        compiler_params=pltpu.CompilerParams(
            dimension_semantics=("arbitrary",), vmem_limit_bytes=VMEM_LIMIT),
        name=f"sample_in_proj_l{layer}",
    )(x, cos, sin, w['g_mix'], w['w_in'])


def _sample_mixer_kernel(layer, n_tok,
                         sinks_ref, q_ref, kn_ref, vn_ref, un_ref, ck_ref, cv_ref, st_ref, w_dw_ref, b_dw_ref,
                         a_ref, cpre_ref, kw_ref, vw_ref, cs_ref, ue_scr):
    n_seq = ck_ref.shape[0]
    rows = n_seq * n_tok
    shift = n_tok.bit_length() - 1
    kc = ck_ref[...].reshape(n_seq * WINDOW, D_KV)
    vc = cv_ref[...].reshape(n_seq * WINDOW, D_KV)
    r = lax.broadcasted_iota(jnp.int32, (rows, n_seq * WINDOW), 0)
    c = lax.broadcasted_iota(jnp.int32, (rows, n_seq * WINDOW), 1)
    mask_cache = ((c >> 7) == (r >> shift)) & ((c & (WINDOW - 1)) >= (r & (n_tok - 1)))
    r = lax.broadcasted_iota(jnp.int32, (rows, rows), 0)
    c = lax.broadcasted_iota(jnp.int32, (rows, rows), 1)
    mask_new = ((c >> shift) == (r >> shift)) & ((c & (n_tok - 1)) <= (r & (n_tok - 1)))
    q_blocks = [q_ref[:, LANES * i:LANES * (i + 1)] for i in range(D_ATTN // LANES)]
    segments = [(kc.astype(BF16), vc.astype(BF16), mask_cache),
                (kn_ref[...].astype(BF16), vn_ref[...].astype(BF16), mask_new)]
    a_blocks = _attention(q_blocks, segments, lambda h: sinks_ref[layer, h])
    for i, ab in enumerate(a_blocks):
        a_ref[:, LANES * i:LANES * (i + 1)] = ab

    for s in range(n_seq):
        new = slice(n_tok * s, n_tok * (s + 1))
        kw_ref[s, 0:WINDOW - n_tok, :] = ck_ref[s, n_tok:WINDOW, :]
        kw_ref[s, WINDOW - n_tok:WINDOW, :] = kn_ref[new, :]
        vw_ref[s, 0:WINDOW - n_tok, :] = cv_ref[s, n_tok:WINDOW, :]
        vw_ref[s, WINDOW - n_tok:WINDOW, :] = vn_ref[new, :]
        ue_scr[s, 0:CONV_HIST, :] = st_ref[s]
        ue_scr[s, CONV_HIST:CONV_HIST + n_tok, :] = un_ref[new, :]
        cs_ref[s] = ue_scr[s, n_tok:n_tok + CONV_HIST, :]
        acc = jnp.zeros((n_tok, 512), F32)
        for j in range(CONV_K):
            acc = acc + ue_scr[s, j:j + n_tok, :] * w_dw_ref[j:j + 1, :]
        cpre_ref[new, :] = acc + b_dw_ref[...]


def _sample_mixer(layer, n_tok, q, kn, vn, un, cache_k, cache_v, state, w):
    n_b = cache_k.shape[1]
    g = SEQS_PER_STEP
    rows = g * n_tok
    lay = lambda *_: (layer, 0, 0)
    row = lambda n: pl.BlockSpec((rows, n), lambda i: (i, 0))
    seq = lambda m, n: pl.BlockSpec((None, g, m, n), lambda i: (layer, i, 0, 0))
    oseq = lambda m, n: pl.BlockSpec((g, m, n), lambda i: (i, 0, 0))
    return pl.pallas_call(
        functools.partial(_sample_mixer_kernel, layer, n_tok),
        grid=(n_b // g,),
        in_specs=[pl.BlockSpec(memory_space=pltpu.SMEM), row(D_ATTN), row(D_KV), row(D_KV), row(512),
                  seq(WINDOW, D_KV), seq(WINDOW, D_KV), seq(CONV_HIST, 512),
                  _resident((None, CONV_K, 512), lay), _resident((None, 1, 512), lay)],
        out_specs=[row(D_ATTN), row(512), oseq(WINDOW, D_KV), oseq(WINDOW, D_KV), oseq(CONV_HIST, 512)],
        out_shape=[jax.ShapeDtypeStruct((n_b * n_tok, D_ATTN), F32),
                   jax.ShapeDtypeStruct((n_b * n_tok, 512), F32),
                   jax.ShapeDtypeStruct((n_b, WINDOW, D_KV), F32),
                   jax.ShapeDtypeStruct((n_b, WINDOW, D_KV), F32),
                   jax.ShapeDtypeStruct((n_b, CONV_HIST, 512), F32)],
        scratch_shapes=[pltpu.VMEM((g, CONV_HIST + CONV_PAD // 2, 512), F32)],
        compiler_params=pltpu.CompilerParams(
            dimension_semantics=("parallel",), vmem_limit_bytes=VMEM_LIMIT),
        name=f"sample_mixer_l{layer}",
    )(w['sinks'], q, kn, vn, un, cache_k, cache_v, state, w['w_dw'], w['b_dw'])


def _rope_tables(pos):
    half = HEAD_DIM // 2
    inv = 1.0 / (ROPE_THETA ** (jnp.arange(half, dtype=F32) / half))
    ang = pos.astype(F32)[:, None] * inv[None, :]
    cos, sin = jnp.cos(ang), jnp.sin(ang)
    return jnp.tile(cos, (1, 4)), jnp.tile(jnp.concatenate([-sin, sin], axis=1), (1, 2))


def kernel(x_prompt, x_sample, p_prompt, p_sample, cache_k, cache_v, state_conv, g_mix, w_in, sinks, w_dw, b_dw,
           ln_g, ln_b, w_pw2, g_attn_out, g_conv_out, w_o, g_ffn, w_up, w_down, g_ple, w_ple_gate, w_ple, g_final):
    depth = w_in.shape[0]
    bp, sp, d = x_prompt.shape
    bs, ts, _ = x_sample.shape
    vec = lambda a: a.reshape(depth, 1, a.shape[-1])
    w = dict(
        sinks=sinks, g_mix=vec(g_mix), w_in=w_in.astype(BF16), w_dw=w_dw, b_dw=vec(b_dw), ln_g=vec(ln_g),
        ln_b=vec(ln_b), w_pw2=w_pw2.astype(BF16), g_attn_out=vec(g_attn_out), g_conv_out=vec(g_conv_out),
        w_o=w_o.astype(BF16), g_ffn=vec(g_ffn), w_up=w_up.astype(BF16), w_down=w_down.astype(BF16),
        g_ple=vec(g_ple), w_ple_gate=w_ple_gate.astype(BF16), w_ple=w_ple.astype(BF16))
    g_fin = g_final.reshape(1, d)

    cos_p, sin_p = _rope_tables(jnp.arange(sp, dtype=jnp.int32))
    pos_s = PAST_LEN + (jnp.arange(bs * ts, dtype=jnp.int32) % ts)
    cos_s, sin_s = _rope_tables(pos_s)

    pp = p_prompt.reshape(depth, bp * sp, -1)
    ps = p_sample.reshape(depth, bs * ts, -1)
    ck = cache_k.reshape(depth, bs, WINDOW, D_KV)
    cv = cache_v.reshape(depth, bs, WINDOW, D_KV)

    yp = x_prompt
    ys = x_sample.reshape(bs * ts, d)
    kp, vp, up, ksl, vsl, usl = [], [], [], [], [], []
    for i in range(depth):
        last = g_fin if i == depth - 1 else None
        q, kn, vn, un = _sample_in_proj(i, ys, cos_s, sin_s, w)
        a, cpre, kw, vw, cs = _sample_mixer(i, ts, q, kn, vn, un, ck, cv, state_conv, w)
        ys = _sample_out_proj(i, ys, a, cpre, w)
        ksl.append(kw); vsl.append(vw); usl.append(cs)

        yp, ys, kt, vt, ut = _layer(i, yp, cos_p, sin_p, pp, ys, ps, w, g_final=last)
        yp = yp.reshape(bp, sp, d)
        kp.append(kt); vp.append(vt); up.append(ut[:, CONV_PAD - CONV_HIST:, :])

    heads = lambda lst, b: jnp.stack(lst).reshape(depth, b, WINDOW, N_KV_HEADS, HEAD_DIM)
    return (yp, ys.reshape(bs, ts, d), heads(kp, bp), heads(vp, bp), jnp.stack(up),
            heads(ksl, bs), heads(vsl, bs), jnp.stack(usl))
```

```python
import functools

import jax
import jax.numpy as jnp
from jax import lax
from jax.experimental import pallas as pl
from jax.experimental.pallas import tpu as pltpu

F32 = jnp.float32
BF16 = jnp.bfloat16

HEAD_DIM = 64
N_Q_HEADS = 8
N_KV_HEADS = 2
Q_PER_KV = N_Q_HEADS // N_KV_HEADS
D_ATTN = N_Q_HEADS * HEAD_DIM
D_KV = N_KV_HEADS * HEAD_DIM
WINDOW = 128
CONV_K = 31
CONV_HIST = CONV_K - 1
PAST_LEN = 8192
ROPE_THETA = 10000.0
EPS = 1e-6
NEG = -1e30

LANES = 128
Q_BLOCK = 128
CONV_PAD = 32
CONV_ROWS = 64
FF_CHUNK = 512
SEQS_PER_STEP = 8
VMEM_LIMIT = 56 * 1024 * 1024


def _rmsnorm(x, g):
    ms = jnp.mean(x * x, axis=-1, keepdims=True)
    return x * lax.rsqrt(ms + EPS) * g


def _dot(a, b):
    return jnp.dot(a, b, preferred_element_type=F32)


def _swap_half_heads(x):
    lane = lax.broadcasted_iota(jnp.int32, x.shape, 1)
    first = (lane & 32) == 0
    return jnp.where(first, pltpu.roll(x, LANES - 32, 1), pltpu.roll(x, 32, 1))


def _rope(x, cos, sin):
    return x * cos + _swap_half_heads(x) * sin


def _in_proj(x, g_mix, w_in_ref, cos, sin):
    h = _rmsnorm(x, g_mix).astype(BF16)
    q = _dot(h, w_in_ref[:, 0:D_ATTN])
    kv = _dot(h, w_in_ref[:, D_ATTN:D_ATTN + 2 * D_KV])
    o = D_ATTN + 2 * D_KV
    ua = _dot(h, w_in_ref[:, o:o + 512])
    ug = _dot(h, w_in_ref[:, o + 512:o + 1024])
    scale = HEAD_DIM ** -0.5
    qs = [_rope(q[:, LANES * c:LANES * (c + 1)], cos, sin) * scale for c in range(D_ATTN // LANES)]
    k = _rope(kv[:, 0:D_KV], cos, sin)
    v = kv[:, D_KV:2 * D_KV]
    u = ua * jax.nn.sigmoid(ug)
    return qs, k, v, u


def _attention(q_blocks, segments, sink_of_head):
    nn = (((1,), (0,)), ((), ()))
    nt = (((1,), (1,)), ((), ()))
    rows = q_blocks[0].shape[0]
    lane = lax.broadcasted_iota(jnp.int32, (rows, LANES), 1)
    low = lane < HEAD_DIM
    q_hat = []
    for h in range(N_Q_HEADS):
        g = h // Q_PER_KV
        blk = q_blocks[h // 2]
        if (h % 2) != g:
            blk = pltpu.roll(blk, HEAD_DIM, 1)
        q_hat.append(jnp.where(low if g == 0 else ~low, blk, 0.0).astype(BF16))
    q_hat = jnp.concatenate(q_hat, axis=0)
    scores = [lax.dot_general(q_hat, k, nn if tr else nt, preferred_element_type=F32)
              for k, _, tr, _ in segments]
    probs = [[] for _ in segments]
    dens = []
    for h in range(N_Q_HEADS):
        sink = sink_of_head(h)
        masked = [jnp.where(m, s[h * rows:(h + 1) * rows], NEG) for s, (_, _, _, m) in zip(scores, segments)]
        mx = sink
        for sm in masked:
            mx = jnp.maximum(mx, jnp.max(sm, axis=-1, keepdims=True))
        den = jnp.exp(sink - mx)
        for i, sm in enumerate(masked):
            e = jnp.exp(sm - mx)
            den = den + jnp.sum(e, axis=-1, keepdims=True)
            probs[i].append(e.astype(BF16))
        dens.append(den)
    out = None
    for i, (_, v, tr, _) in enumerate(segments):
        o = lax.dot_general(jnp.concatenate(probs[i], axis=0), v, nt if tr else nn,
                            preferred_element_type=F32)
        out = o if out is None else out + o
    placed = []
    for h in range(N_Q_HEADS):
        g = h // Q_PER_KV
        o = out[h * rows:(h + 1) * rows] / dens[h]
        if (h % 2) != g:
            o = pltpu.roll(o, HEAD_DIM, 1)
        placed.append(o)
    return [jnp.where(low, placed[2 * c], placed[2 * c + 1]) for c in range(D_ATTN // LANES)]


def _post_mix(a, cpre, ln_g, ln_b, w_pw2_ref, g_ao, g_co, w_o_ref):
    mu = jnp.mean(cpre, axis=-1, keepdims=True)
    xc = cpre - mu
    var = jnp.mean(xc * xc, axis=-1, keepdims=True)
    y = xc * lax.rsqrt(var + EPS) * ln_g + ln_b
    y = y * jax.nn.sigmoid(y)
    c = _dot(y.astype(BF16), w_pw2_ref[...])
    an = _rmsnorm(a, g_ao).astype(BF16)
    cn = _rmsnorm(c, g_co).astype(BF16)
    return _dot(an, w_o_ref[0:D_ATTN, :]) + _dot(cn, w_o_ref[D_ATTN:, :])


def _resident(shape, index_map):
    return pl.BlockSpec(shape, index_map, pipeline_mode=pl.Buffered(1))


def _layer_kernel(layer, tm, n_s, n_tiles, final, *refs):
    (sinks_ref, x_ref, cos_ref, sin_ref, p_ref, xs_ref, ps_ref, g_mix_ref, w_in_ref, w_dw_ref, b_dw_ref,
     ln_g_ref, ln_b_ref, w_pw2_ref, g_ao_ref, g_co_ref, w_o_ref, g_ffn_ref, w_up_ref, w_down_ref, g_ple_ref,
     w_gate_ref, w_ple_ref) = refs[:23]
    refs = refs[23:]
    g_final = None
    if final:
        g_final = refs[0][...]
        refs = refs[1:]
    (y_ref, ys_ref, k_tail_ref, v_tail_ref, u_tail_ref,
     x1_scr, q_scr, kt_scr, vs_scr, kt_hist, vs_hist, u_scr, a_scr, c_scr) = refs

    t = pl.program_id(0)
    slot = t & 1
    s_idx = lax.rem(jnp.minimum(t, n_tiles - 1), n_s)
    lane = lax.broadcasted_iota(jnp.int32, (1, LANES), 1)
    half = [lane < HEAD_DIM, lane >= HEAD_DIM]
    ones_half = [jnp.where(half[p], 1.0, 0.0).astype(BF16) for p in range(2)]

    @pl.when(t == 0)
    def _():
        x1_scr[1] = xs_ref[...]
        for g in range(N_KV_HEADS):
            for p in range(2):
                vs_scr[g, p, :, LANES:] = jnp.broadcast_to(ones_half[p], (tm, LANES))
                for sl in range(2):
                    vs_hist[sl, g, p, :, LANES:] = jnp.broadcast_to(ones_half[p], (WINDOW, LANES))

    @pl.when(s_idx == 0)
    def _():
        for g in range(N_KV_HEADS):
            for p in range(2):
                kt_hist[slot, g, p] = jnp.zeros((WINDOW, LANES), BF16)
                vs_hist[slot, g, p, :, 0:LANES] = jnp.zeros((WINDOW, LANES), BF16)
        for c in range(512 // LANES):
            u_scr[c, pl.ds(0, CONV_PAD, stride=2), :] = jnp.zeros((CONV_PAD, LANES), F32)

    xf = x1_scr[1 - slot]
    xfn = _rmsnorm(xf, g_ffn_ref[...]).astype(BF16)
    ffn_acc = None

    x = x_ref[...]
    qs, k, v, u = _in_proj(x, g_mix_ref[...], w_in_ref, cos_ref[...], sin_ref[...])
    for c, qc in enumerate(qs):
        q_scr[:, LANES * c:LANES * (c + 1)] = qc.astype(BF16)
    k_swapped, v_swapped = pltpu.roll(k, HEAD_DIM, 1), pltpu.roll(v, HEAD_DIM, 1)
    for g in range(N_KV_HEADS):
        for p in range(2):
            kt_scr[g, p] = jnp.where(half[p], k if p == g else k_swapped, 0.0).astype(BF16)
            vs_scr[g, p, :, 0:LANES] = jnp.where(half[p], v if p == g else v_swapped, 0.0).astype(BF16)
    for c in range(512 // LANES):
        u_scr[c, pl.ds(2 * CONV_PAD, tm, stride=2), :] = u[:, LANES * c:LANES * (c + 1)]
    k_tail_ref[0] = k[tm - WINDOW:, :]
    v_tail_ref[0] = v[tm - WINDOW:, :]
    u_tail_ref[0] = u[tm - CONV_PAD:, :]

    r = lax.broadcasted_iota(jnp.int32, (Q_BLOCK, Q_BLOCK + WINDOW), 0)
    l = lax.broadcasted_iota(jnp.int32, (Q_BLOCK, Q_BLOCK + WINDOW), 1)
    band = (l >= r) & (l <= r + WINDOW)
    first_band = band & (l + (s_idx * tm - WINDOW) >= 0)
    nt = (((1,), (1,)), ((), ()))
    for unit in range(w_up_ref.shape[1] // FF_CHUNK):
        hc = _dot(xfn, w_up_ref[:, FF_CHUNK * unit:FF_CHUNK * (unit + 1)])
        hc = jnp.square(jnp.maximum(hc, 0.0)).astype(BF16)
        part = _dot(hc, w_down_ref[FF_CHUNK * unit:FF_CHUNK * (unit + 1), :])
        ffn_acc = part if ffn_acc is None else ffn_acc + part

        r0 = CONV_ROWS * unit
        for c in range(512 // LANES):
            cols = slice(LANES * c, LANES * (c + 1))
            conv = jnp.zeros((CONV_ROWS, LANES), F32)
            for tap in range(CONV_K):
                lo = r0 + (CONV_PAD - CONV_HIST) + tap
                conv = conv + u_scr[c, pl.ds(2 * lo, CONV_ROWS, stride=2), :] * w_dw_ref[tap:tap + 1, cols]
            c_scr[r0:r0 + CONV_ROWS, cols] = conv + b_dw_ref[:, cols]

        j, g = divmod(unit, N_KV_HEADS)
        rows = slice(Q_BLOCK * j, Q_BLOCK * (j + 1))
        mask = first_band if j == 0 else band

        def keys_of(cur, hist, p):
            if j == 0:
                return jnp.concatenate([hist[slot, g, p], cur[g, p, 0:Q_BLOCK]], axis=0)
            return cur[g, p, Q_BLOCK * j - WINDOW:Q_BLOCK * (j + 1)]

        q_pair = jnp.concatenate([q_scr[rows, LANES * c:LANES * (c + 1)] for c in (2 * g, 2 * g + 1)], axis=0)
        probs, sink_terms = [], {}
        for p in range(2):
            s = lax.dot_general(q_pair, keys_of(kt_scr, kt_hist, p), nt, preferred_element_type=F32)
            es = []
            for i in range(2):
                sink = sinks_ref[layer, 4 * g + 2 * i + p]
                sm = jnp.where(mask, s[Q_BLOCK * i:Q_BLOCK * (i + 1)], NEG)
                mx = jnp.maximum(jnp.max(sm, axis=-1, keepdims=True), sink)
                es.append(jnp.exp(sm - mx).astype(BF16))
                sink_terms[i, p] = jnp.exp(sink - mx)
            probs.append(jnp.concatenate(es, axis=0))
        acc = _dot(probs[0], keys_of(vs_scr, vs_hist, 0)) + _dot(probs[1], keys_of(vs_scr, vs_hist, 1))
        for i in range(2):
            c = 2 * g + i
            num = acc[Q_BLOCK * i:Q_BLOCK * (i + 1), 0:LANES]
            den = acc[Q_BLOCK * i:Q_BLOCK * (i + 1), LANES:] + jnp.where(half[0], sink_terms[i, 0], sink_terms[i, 1])
            a_scr[rows, LANES * c:LANES * (c + 1)] = num / den

    xf = xf + ffn_acc
    gate = jax.nn.sigmoid(_dot(_rmsnorm(xf, g_ple_ref[...]).astype(BF16), w_gate_ref[...]))
    p_rows = jnp.where(t == 0, ps_ref[...], p_ref[...])
    xf = xf + _dot(p_rows.astype(BF16), w_ple_ref[...]) * gate
    if g_final is not None:
        xf = _rmsnorm(xf, g_final)
    y_ref[...] = xf

    mix = _post_mix(a_scr[...], c_scr[...], ln_g_ref[...], ln_b_ref[...], w_pw2_ref,
                    g_ao_ref[...], g_co_ref[...], w_o_ref)
    x1_scr[slot] = x + mix

    for g in range(N_KV_HEADS):
        for p in range(2):
            kt_hist[1 - slot, g, p] = kt_scr[g, p, tm - WINDOW:tm, :]
            vs_hist[1 - slot, g, p, :, 0:LANES] = vs_scr[g, p, tm - WINDOW:tm, 0:LANES]
    for c in range(512 // LANES):
        u_scr[c, pl.ds(0, CONV_PAD, stride=2), :] = u_scr[c, pl.ds(2 * tm, CONV_PAD, stride=2), :]

    @pl.when(t == 0)
    def _():
        ys_ref[...] = y_ref[...]


def _layer(layer, x, cos, sin, p, xs, ps, w, g_final=None):
    b, s, d = x.shape
    tm = xs.shape[0]
    n_s = s // tm
    n_tiles = b * n_s
    assert w['w_up'].shape[-1] // FF_CHUNK == (tm // Q_BLOCK) * N_KV_HEADS == tm // CONV_ROWS
    lay = lambda *_: (layer, 0, 0)
    vec = lambda n: _resident((None, 1, n), lay)
    mixer_tile = lambda t: jnp.minimum(t, n_tiles - 1)
    ffn_tile = lambda t: jnp.maximum(t - 1, 0)
    seq_of = lambda t: mixer_tile(t) // n_s
    in_specs = [
        pl.BlockSpec(memory_space=pltpu.SMEM),
        pl.BlockSpec((tm, d), lambda t: (mixer_tile(t), 0)),
        pl.BlockSpec((tm, LANES), lambda t: (lax.rem(mixer_tile(t), n_s), 0)),
        pl.BlockSpec((tm, LANES), lambda t: (lax.rem(mixer_tile(t), n_s), 0)),
        pl.BlockSpec((None, tm, p.shape[-1]), lambda t: (layer, ffn_tile(t), 0)),
        _resident((tm, d), lambda t: (0, 0)),
        _resident((None, tm, p.shape[-1]), lay),
        vec(d),
        _resident((None,) + w['w_in'].shape[1:], lay),
        _resident((None, CONV_K, 512), lay),
        vec(512), vec(512), vec(512),
        _resident((None, 512, 512), lay),
        vec(512), vec(512),
        _resident((None, d, d), lay),
        vec(d),
        _resident((None,) + w['w_up'].shape[1:], lay), _resident((None,) + w['w_down'].shape[1:], lay),
        vec(d),
        _resident((None, d, d), lay), _resident((None,) + w['w_ple'].shape[1:], lay),
    ]
    args = [w['sinks'], x.reshape(b * s, d), cos, sin, p, xs, ps, w['g_mix'], w['w_in'], w['w_dw'], w['b_dw'],
            w['ln_g'], w['ln_b'], w['w_pw2'], w['g_attn_out'], w['g_conv_out'], w['w_o'], w['g_ffn'], w['w_up'],
            w['w_down'], w['g_ple'], w['w_ple_gate'], w['w_ple']]
    if g_final is not None:
        args.append(g_final)
        in_specs.append(_resident((1, d), lambda t: (0, 0)))
    out_shape = [
        jax.ShapeDtypeStruct((b * s, d), F32),
        jax.ShapeDtypeStruct((tm, d), F32),
        jax.ShapeDtypeStruct((b, WINDOW, D_KV), F32),
        jax.ShapeDtypeStruct((b, WINDOW, D_KV), F32),
        jax.ShapeDtypeStruct((b, CONV_PAD, 512), F32),
    ]
    out_specs = [
        pl.BlockSpec((tm, d), lambda t: (ffn_tile(t), 0)),
        pl.BlockSpec((tm, d), lambda t: (0, 0)),
        pl.BlockSpec((1, WINDOW, D_KV), lambda t: (seq_of(t), 0, 0)),
        pl.BlockSpec((1, WINDOW, D_KV), lambda t: (seq_of(t), 0, 0)),
        pl.BlockSpec((1, CONV_PAD, 512), lambda t: (seq_of(t), 0, 0)),
    ]
    scratch = [
        pltpu.VMEM((2, tm, d), F32),
        pltpu.VMEM((tm, D_ATTN), BF16),
        pltpu.VMEM((N_KV_HEADS, 2, tm, LANES), BF16),
        pltpu.VMEM((N_KV_HEADS, 2, tm, 2 * LANES), BF16),
        pltpu.VMEM((2, N_KV_HEADS, 2, WINDOW, LANES), BF16),
        pltpu.VMEM((2, N_KV_HEADS, 2, WINDOW, 2 * LANES), BF16),
        pltpu.VMEM((512 // LANES, 2 * (tm + CONV_PAD), LANES), F32),
        pltpu.VMEM((tm, D_ATTN), F32),
        pltpu.VMEM((tm, 512), F32),
    ]
    return pl.pallas_call(
        functools.partial(_layer_kernel, layer, tm, n_s, n_tiles, g_final is not None),
        grid=(n_tiles + 1,),
        in_specs=in_specs, out_specs=out_specs, out_shape=out_shape, scratch_shapes=scratch,
        compiler_params=pltpu.CompilerParams(
            dimension_semantics=("arbitrary",), vmem_limit_bytes=VMEM_LIMIT),
        name=f"layer_l{layer}",
    )(*args)


def _sample_out_proj_kernel(x_ref, a_ref, cpre_ref, ln_g_ref, ln_b_ref, w_pw2_ref, g_ao_ref, g_co_ref, w_o_ref,
                            out_ref):
    out_ref[...] = x_ref[...] + _post_mix(a_ref[...], cpre_ref[...], ln_g_ref[...], ln_b_ref[...], w_pw2_ref,
                                          g_ao_ref[...], g_co_ref[...], w_o_ref)


def _sample_out_proj(layer, x, a, cpre, w):
    t, d = x.shape
    lay = lambda *_: (layer, 0, 0)
    vec = lambda n: _resident((None, 1, n), lay)
    row = lambda n: pl.BlockSpec((t, n), lambda i: (0, 0))
    return pl.pallas_call(
        _sample_out_proj_kernel,
        grid=(1,),
        in_specs=[row(d), row(512), row(512), vec(512), vec(512), _resident((None, 512, 512), lay),
                  vec(512), vec(512), _resident((None, d, d), lay)],
        out_specs=row(d), out_shape=jax.ShapeDtypeStruct((t, d), F32),
        compiler_params=pltpu.CompilerParams(
            dimension_semantics=("arbitrary",), vmem_limit_bytes=VMEM_LIMIT),
        name=f"sample_out_proj_l{layer}",
    )(x, a, cpre, w['ln_g'], w['ln_b'], w['w_pw2'], w['g_attn_out'], w['g_conv_out'], w['w_o'])


def _sample_in_proj_kernel(x_ref, cos_ref, sin_ref, g_mix_ref, w_in_ref, q_ref, k_ref, v_ref, u_ref):
    qs, k, v, u = _in_proj(x_ref[...], g_mix_ref[...], w_in_ref, cos_ref[...], sin_ref[...])
    for c, qc in enumerate(qs):
        q_ref[:, LANES * c:LANES * (c + 1)] = qc
    k_ref[...] = k
    v_ref[...] = v
    u_ref[...] = u


def _sample_in_proj(layer, x, cos, sin, w):
    t, d = x.shape
    lay = lambda *_: (layer, 0, 0)
    full = lambda n: pl.BlockSpec((t, n), lambda i: (0, 0))
    return pl.pallas_call(
        _sample_in_proj_kernel,
        grid=(1,),
        in_specs=[full(d), full(LANES), full(LANES), _resident((None, 1, d), lay),
                  _resident((None,) + w['w_in'].shape[1:], lay)],
        out_specs=[full(D_ATTN), full(D_KV), full(D_KV), full(512)],
        out_shape=[jax.ShapeDtypeStruct((t, n), F32) for n in (D_ATTN, D_KV, D_KV, 512)],
        compiler_params=pltpu.CompilerParams(
            dimension_semantics=("arbitrary",), vmem_limit_bytes=VMEM_LIMIT),
        name=f"sample_in_proj_l{layer}",
    )(x, cos, sin, w['g_mix'], w['w_in'])


def _sample_mixer_kernel(layer, n_tok,
                         sinks_ref, q_ref, kn_ref, vn_ref, un_ref, ckt_ref, cvt_ref, st_ref, w_dw_ref, b_dw_ref,
                         kw_hbm, vw_hbm, cs_hbm,
                         a_ref, cpre_ref, kw_ref, vw_ref, cs_ref, un_scr, cp_scr):
    del kw_hbm, vw_hbm, cs_hbm
    n_seq = ckt_ref.shape[0]
    rows = n_seq * n_tok
    shift = n_tok.bit_length() - 1
    kct = jnp.concatenate([ckt_ref[s] for s in range(n_seq)], axis=1).astype(BF16)
    vct = jnp.concatenate([cvt_ref[s] for s in range(n_seq)], axis=1).astype(BF16)
    r = lax.broadcasted_iota(jnp.int32, (rows, n_seq * WINDOW), 0)
    c = lax.broadcasted_iota(jnp.int32, (rows, n_seq * WINDOW), 1)
    mask_cache = ((c >> 7) == (r >> shift)) & ((c & (WINDOW - 1)) >= (r & (n_tok - 1)))
    r = lax.broadcasted_iota(jnp.int32, (rows, rows), 0)
    c = lax.broadcasted_iota(jnp.int32, (rows, rows), 1)
    mask_new = ((c >> shift) == (r >> shift)) & ((c & (n_tok - 1)) <= (r & (n_tok - 1)))
    q_blocks = [q_ref[:, LANES * i:LANES * (i + 1)] for i in range(D_ATTN // LANES)]
    kn, vn = kn_ref[...], vn_ref[...]
    segments = [(kct, vct, True, mask_cache), (kn.astype(BF16), vn.astype(BF16), False, mask_new)]
    a_blocks = _attention(q_blocks, segments, lambda h: sinks_ref[layer, h])
    for i, ab in enumerate(a_blocks):
        a_ref[:, LANES * i:LANES * (i + 1)] = ab

    pad = jnp.zeros((WINDOW - rows, D_KV), F32)
    lane = lax.broadcasted_iota(jnp.int32, (D_KV, WINDOW), 1)
    tail = lane >= WINDOW - n_tok
    for new, old_ref, out_ref in ((kn, ckt_ref, kw_ref), (vn, cvt_ref, vw_ref)):
        new_t = jnp.concatenate([new, pad], axis=0).T
        for s in range(n_seq):
            shifted = pltpu.roll(old_ref[s], WINDOW - n_tok, 1)
            placed = pltpu.roll(new_t, (WINDOW - n_tok - n_tok * s) % WINDOW, 1)
            out_ref[s] = jnp.where(tail, placed, shifted)

    for c in range(512 // LANES):
        un_scr[c] = un_ref[:, LANES * c:LANES * (c + 1)]
    new_planes = [jnp.concatenate([un_scr[c, pl.ds(j, n_seq, stride=n_tok), :] for c in range(512 // LANES)], axis=1)
                  for j in range(n_tok)]
    plane = lambda i: st_ref[i] if i < CONV_HIST else new_planes[i - CONV_HIST]
    for j in range(n_tok):
        acc = jnp.zeros((n_seq, 512), F32)
        for tap in range(CONV_K):
            acc = acc + plane(j + tap) * w_dw_ref[tap:tap + 1, :]
        acc = acc + b_dw_ref[...]
        for c in range(512 // LANES):
            cp_scr[c, pl.ds(j, n_seq, stride=n_tok), :] = acc[:, LANES * c:LANES * (c + 1)]
    for c in range(512 // LANES):
        cpre_ref[:, LANES * c:LANES * (c + 1)] = cp_scr[c]
    for i in range(CONV_HIST):
        cs_ref[i] = plane(i + n_tok)


def _sample_mixer(layer, n_tok, q, kn, vn, un, cache_kt, cache_vt, state_t, windows, w):
    n_b = cache_kt.shape[1]
    g = SEQS_PER_STEP
    rows = g * n_tok
    lay = lambda *_: (layer, 0, 0)
    row = lambda n: pl.BlockSpec((rows, n), lambda i: (i, 0))
    seq = pl.BlockSpec((None, g, D_KV, WINDOW), lambda i: (layer, i, 0, 0))
    planes = pl.BlockSpec((None, CONV_HIST, g, 512), lambda i: (layer, 0, i, 0))
    hbm = pl.BlockSpec(memory_space=pl.ANY)
    args = [w['sinks'], q, kn, vn, un, cache_kt, cache_vt, state_t, w['w_dw'], w['b_dw'], *windows]
    first_window = len(args) - len(windows)
    return pl.pallas_call(
        functools.partial(_sample_mixer_kernel, layer, n_tok),
        grid=(n_b // g,),
        in_specs=[pl.BlockSpec(memory_space=pltpu.SMEM), row(D_ATTN), row(D_KV), row(D_KV), row(512),
                  seq, seq, planes, _resident((None, CONV_K, 512), lay), _resident((None, 1, 512), lay),
                  hbm, hbm, hbm],
        out_specs=[row(D_ATTN), row(512), seq, seq, planes],
        out_shape=[jax.ShapeDtypeStruct((n_b * n_tok, D_ATTN), F32),
                   jax.ShapeDtypeStruct((n_b * n_tok, 512), F32)]
                  + [jax.ShapeDtypeStruct(b.shape, F32) for b in windows],
        input_output_aliases={first_window + i: 2 + i for i in range(len(windows))},
        scratch_shapes=[pltpu.VMEM((512 // LANES, rows, LANES), F32), pltpu.VMEM((512 // LANES, rows, LANES), F32)],
        compiler_params=pltpu.CompilerParams(
            dimension_semantics=("parallel",), vmem_limit_bytes=VMEM_LIMIT),
        name=f"sample_mixer_l{layer}",
    )(*args)


def _rope_tables(pos):
    half = HEAD_DIM // 2
    inv = 1.0 / (ROPE_THETA ** (jnp.arange(half, dtype=F32) / half))
    ang = pos.astype(F32)[:, None] * inv[None, :]
    cos, sin = jnp.cos(ang), jnp.sin(ang)
    return jnp.tile(cos, (1, 4)), jnp.tile(jnp.concatenate([-sin, sin], axis=1), (1, 2))


def kernel(x_prompt, x_sample, p_prompt, p_sample, cache_k, cache_v, state_conv, g_mix, w_in, sinks, w_dw, b_dw,
           ln_g, ln_b, w_pw2, g_attn_out, g_conv_out, w_o, g_ffn, w_up, w_down, g_ple, w_ple_gate, w_ple, g_final):
    depth = w_in.shape[0]
    bp, sp, d = x_prompt.shape
    bs, ts, _ = x_sample.shape
    vec = lambda a: a.reshape(depth, 1, a.shape[-1])
    w = dict(
        sinks=sinks, g_mix=vec(g_mix), w_in=w_in.astype(BF16), w_dw=w_dw, b_dw=vec(b_dw), ln_g=vec(ln_g),
        ln_b=vec(ln_b), w_pw2=w_pw2.astype(BF16), g_attn_out=vec(g_attn_out), g_conv_out=vec(g_conv_out),
        w_o=w_o.astype(BF16), g_ffn=vec(g_ffn), w_up=w_up.astype(BF16), w_down=w_down.astype(BF16),
        g_ple=vec(g_ple), w_ple_gate=w_ple_gate.astype(BF16), w_ple=w_ple.astype(BF16))
    g_fin = g_final.reshape(1, d)

    cos_p, sin_p = _rope_tables(jnp.arange(sp, dtype=jnp.int32))
    pos_s = PAST_LEN + (jnp.arange(bs * ts, dtype=jnp.int32) % ts)
    cos_s, sin_s = _rope_tables(pos_s)

    pp = p_prompt.reshape(depth, bp * sp, -1)
    ps = p_sample.reshape(depth, bs * ts, -1)
    feature_major = lambda c: c.transpose(0, 1, 3, 4, 2).reshape(depth, bs, D_KV, WINDOW)
    ckt, cvt = feature_major(cache_k), feature_major(cache_v)
    st = state_conv.transpose(0, 2, 1, 3)
    windows = (jnp.zeros_like(ckt), jnp.zeros_like(cvt), jnp.zeros_like(st))

    yp = x_prompt
    ys = x_sample.reshape(bs * ts, d)
    kp, vp, up = [], [], []
    for i in range(depth):
        last = g_fin if i == depth - 1 else None
        q, kn, vn, un = _sample_in_proj(i, ys, cos_s, sin_s, w)
        a, cpre, *windows = _sample_mixer(i, ts, q, kn, vn, un, ckt, cvt, st, windows, w)
        ys = _sample_out_proj(i, ys, a, cpre, w)

        yp, ys, kt, vt, ut = _layer(i, yp, cos_p, sin_p, pp, ys, ps, w, g_final=last)
        yp = yp.reshape(bp, sp, d)
        kp.append(kt); vp.append(vt); up.append(ut[:, CONV_PAD - CONV_HIST:, :])

    heads = lambda lst, b: jnp.stack(lst).reshape(depth, b, WINDOW, N_KV_HEADS, HEAD_DIM)
    window_major = lambda c: c.reshape(depth, bs, N_KV_HEADS, HEAD_DIM, WINDOW).transpose(0, 1, 4, 2, 3)
    kw, vw, cs = windows
    return (yp, ys.reshape(bs, ts, d), heads(kp, bp), heads(vp, bp), jnp.stack(up),
            window_major(kw), window_major(vw), cs.transpose(0, 2, 1, 3))
```

```python
import functools

import jax
import jax.numpy as jnp
from jax import lax
from jax.experimental import pallas as pl
from jax.experimental.pallas import tpu as pltpu

F32 = jnp.float32
BF16 = jnp.bfloat16

HEAD_DIM = 64
N_Q_HEADS = 8
N_KV_HEADS = 2
Q_PER_KV = N_Q_HEADS // N_KV_HEADS
D_ATTN = N_Q_HEADS * HEAD_DIM
D_KV = N_KV_HEADS * HEAD_DIM
WINDOW = 128
CONV_K = 31
CONV_HIST = CONV_K - 1
PAST_LEN = 8192
ROPE_THETA = 10000.0
EPS = 1e-6
NEG = -1e30

LANES = 128
Q_BLOCK = 128
CONV_PAD = 32
CONV_ROWS = 64
FF_CHUNK = 1024
SEQS_PER_STEP = 8
VMEM_LIMIT = 56 * 1024 * 1024


def _rmsnorm(x, g):
    ms = jnp.mean(x * x, axis=-1, keepdims=True)
    return x * lax.rsqrt(ms + EPS) * g


def _dot(a, b):
    return jnp.dot(a, b, preferred_element_type=F32)


def _swap_half_heads(x):
    lane = lax.broadcasted_iota(jnp.int32, x.shape, 1)
    first = (lane & 32) == 0
    return jnp.where(first, pltpu.roll(x, LANES - 32, 1), pltpu.roll(x, 32, 1))


def _rope(x, cos, sin):
    return x * cos + _swap_half_heads(x) * sin


def _in_proj(x, g_mix, w_in_ref, cos, sin):
    h = _rmsnorm(x, g_mix).astype(BF16)
    q = _dot(h, w_in_ref[:, 0:D_ATTN])
    kv = _dot(h, w_in_ref[:, D_ATTN:D_ATTN + 2 * D_KV])
    o = D_ATTN + 2 * D_KV
    ua = _dot(h, w_in_ref[:, o:o + 512])
    ug = _dot(h, w_in_ref[:, o + 512:o + 1024])
    scale = HEAD_DIM ** -0.5
    qs = [_rope(q[:, LANES * c:LANES * (c + 1)], cos, sin) * scale for c in range(D_ATTN // LANES)]
    k = _rope(kv[:, 0:D_KV], cos, sin)
    v = kv[:, D_KV:2 * D_KV]
    u = ua * jax.nn.sigmoid(ug)
    return qs, k, v, u


def _attention(q_blocks, segments, sink_of_head):
    nn = (((1,), (0,)), ((), ()))
    nt = (((1,), (1,)), ((), ()))
    rows = q_blocks[0].shape[0]
    lane = lax.broadcasted_iota(jnp.int32, (rows, LANES), 1)
    low = lane < HEAD_DIM
    q_hat = []
    for h in range(N_Q_HEADS):
        g = h // Q_PER_KV
        blk = q_blocks[h // 2]
        if (h % 2) != g:
            blk = pltpu.roll(blk, HEAD_DIM, 1)
        q_hat.append(jnp.where(low if g == 0 else ~low, blk, 0.0).astype(BF16))
    q_hat = jnp.concatenate(q_hat, axis=0)
    scores = [lax.dot_general(q_hat, k, nn if tr else nt, preferred_element_type=F32)
              for k, _, tr, _ in segments]
    probs = [[] for _ in segments]
    dens = []
    for h in range(N_Q_HEADS):
        sink = sink_of_head(h)
        masked = [jnp.where(m, s[h * rows:(h + 1) * rows], NEG) for s, (_, _, _, m) in zip(scores, segments)]
        mx = sink
        for sm in masked:
            mx = jnp.maximum(mx, jnp.max(sm, axis=-1, keepdims=True))
        den = jnp.exp(sink - mx)
        for i, sm in enumerate(masked):
            e = jnp.exp(sm - mx)
            den = den + jnp.sum(e, axis=-1, keepdims=True)
            probs[i].append(e.astype(BF16))
        dens.append(den)
    out = None
    for i, (_, v, tr, _) in enumerate(segments):
        o = lax.dot_general(jnp.concatenate(probs[i], axis=0), v, nt if tr else nn,
                            preferred_element_type=F32)
        out = o if out is None else out + o
    placed = []
    for h in range(N_Q_HEADS):
        g = h // Q_PER_KV
        o = out[h * rows:(h + 1) * rows] / dens[h]
        if (h % 2) != g:
            o = pltpu.roll(o, HEAD_DIM, 1)
        placed.append(o)
    return [jnp.where(low, placed[2 * c], placed[2 * c + 1]) for c in range(D_ATTN // LANES)]


def _post_mix(a, cpre, ln_g, ln_b, w_pw2_ref, g_ao, g_co, w_o_ref):
    mu = jnp.mean(cpre, axis=-1, keepdims=True)
    xc = cpre - mu
    var = jnp.mean(xc * xc, axis=-1, keepdims=True)
    y = xc * lax.rsqrt(var + EPS) * ln_g + ln_b
    y = y * jax.nn.sigmoid(y)
    c = _dot(y.astype(BF16), w_pw2_ref[...])
    an = _rmsnorm(a, g_ao).astype(BF16)
    cn = _rmsnorm(c, g_co).astype(BF16)
    return _dot(an, w_o_ref[0:D_ATTN, :]) + _dot(cn, w_o_ref[D_ATTN:, :])


def _resident(shape, index_map):
    return pl.BlockSpec(shape, index_map, pipeline_mode=pl.Buffered(1))


def _layer_kernel(layer, tm, n_s, n_tiles, final, *refs):
    (sinks_ref, x_ref, cos_ref, sin_ref, p_ref, xs_ref, ps_ref, g_mix_ref, w_in_ref, w_dw_ref, b_dw_ref,
     ln_g_ref, ln_b_ref, w_pw2_ref, g_ao_ref, g_co_ref, w_o_ref, g_ffn_ref, w_up_ref, w_down_ref, g_ple_ref,
     w_gate_ref, w_ple_ref) = refs[:23]
    refs = refs[23:]
    g_final = None
    if final:
        g_final = refs[0][...]
        refs = refs[1:]
    (y_ref, ys_ref, k_tail_ref, v_tail_ref, u_tail_ref,
     x1_scr, q_scr, kt_scr, vs_scr, kt_hist, vs_hist, u_scr, a_scr, c_scr) = refs

    t = pl.program_id(0)
    slot = t & 1
    s_idx = lax.rem(jnp.minimum(t, n_tiles - 1), n_s)
    lane = lax.broadcasted_iota(jnp.int32, (1, LANES), 1)
    half = [lane < HEAD_DIM, lane >= HEAD_DIM]
    ones_half = [jnp.where(half[p], 1.0, 0.0).astype(BF16) for p in range(2)]

    @pl.when(t == 0)
    def _():
        x1_scr[1] = xs_ref[...]
        for g in range(N_KV_HEADS):
            for p in range(2):
                vs_scr[g, p, :, LANES:] = jnp.broadcast_to(ones_half[p], (tm, LANES))
                for sl in range(2):
                    vs_hist[sl, g, p, :, LANES:] = jnp.broadcast_to(ones_half[p], (WINDOW, LANES))

    @pl.when(s_idx == 0)
    def _():
        for g in range(N_KV_HEADS):
            for p in range(2):
                kt_hist[slot, g, p] = jnp.zeros((WINDOW, LANES), BF16)
                vs_hist[slot, g, p, :, 0:LANES] = jnp.zeros((WINDOW, LANES), BF16)
        for c in range(512 // LANES):
            u_scr[c, pl.ds(0, CONV_PAD, stride=2), :] = jnp.zeros((CONV_PAD, LANES), F32)

    xf = x1_scr[1 - slot]
    xfn = _rmsnorm(xf, g_ffn_ref[...]).astype(BF16)

    x = x_ref[...]
    qs, k, v, u = _in_proj(x, g_mix_ref[...], w_in_ref, cos_ref[...], sin_ref[...])
    for c, qc in enumerate(qs):
        q_scr[:, LANES * c:LANES * (c + 1)] = qc.astype(BF16)
    k_swapped, v_swapped = pltpu.roll(k, HEAD_DIM, 1), pltpu.roll(v, HEAD_DIM, 1)
    for g in range(N_KV_HEADS):
        for p in range(2):
            kt_scr[g, p] = jnp.where(half[p], k if p == g else k_swapped, 0.0).astype(BF16)
            vs_scr[g, p, :, 0:LANES] = jnp.where(half[p], v if p == g else v_swapped, 0.0).astype(BF16)
    for c in range(512 // LANES):
        u_scr[c, pl.ds(2 * CONV_PAD, tm, stride=2), :] = u[:, LANES * c:LANES * (c + 1)]
    k_tail_ref[0] = k[tm - WINDOW:, :]
    v_tail_ref[0] = v[tm - WINDOW:, :]
    u_tail_ref[0] = u[tm - CONV_PAD:, :]

    r = lax.broadcasted_iota(jnp.int32, (Q_BLOCK, Q_BLOCK + WINDOW), 0)
    l = lax.broadcasted_iota(jnp.int32, (Q_BLOCK, Q_BLOCK + WINDOW), 1)
    band = (l >= r) & (l <= r + WINDOW)
    first_band = band & (l + (s_idx * tm - WINDOW) >= 0)
    nt = (((1,), (1,)), ((), ()))
    ffn_parts = []

    def ffn_chunk(unit):
        hc = _dot(xfn, w_up_ref[:, FF_CHUNK * unit:FF_CHUNK * (unit + 1)])
        hc = jnp.square(jnp.maximum(hc, 0.0)).astype(BF16)
        ffn_parts.append(_dot(hc, w_down_ref[FF_CHUNK * unit:FF_CHUNK * (unit + 1), :]))

    def conv_chunk(unit):
        r0 = CONV_ROWS * unit
        for c in range(512 // LANES):
            cols = slice(LANES * c, LANES * (c + 1))
            conv = jnp.zeros((CONV_ROWS, LANES), F32)
            for tap in range(CONV_K):
                lo = r0 + (CONV_PAD - CONV_HIST) + tap
                conv = conv + u_scr[c, pl.ds(2 * lo, CONV_ROWS, stride=2), :] * w_dw_ref[tap:tap + 1, cols]
            c_scr[r0:r0 + CONV_ROWS, cols] = conv + b_dw_ref[:, cols]

    def attention_unit(unit):
        j, g = divmod(unit, N_KV_HEADS)
        rows = slice(Q_BLOCK * j, Q_BLOCK * (j + 1))
        mask = first_band if j == 0 else band

        def keys_of(cur, hist, p):
            if j == 0:
                return jnp.concatenate([hist[slot, g, p], cur[g, p, 0:Q_BLOCK]], axis=0)
            return cur[g, p, Q_BLOCK * j - WINDOW:Q_BLOCK * (j + 1)]

        q_pair = jnp.concatenate([q_scr[rows, LANES * c:LANES * (c + 1)] for c in (2 * g, 2 * g + 1)], axis=0)
        probs, sink_terms = [], {}
        for p in range(2):
            s = lax.dot_general(q_pair, keys_of(kt_scr, kt_hist, p), nt, preferred_element_type=F32)
            es = []
            for i in range(2):
                sink = sinks_ref[layer, 4 * g + 2 * i + p]
                sm = jnp.where(mask, s[Q_BLOCK * i:Q_BLOCK * (i + 1)], NEG)
                mx = jnp.maximum(jnp.max(sm, axis=-1, keepdims=True), sink)
                es.append(jnp.exp(sm - mx).astype(BF16))
                sink_terms[i, p] = jnp.exp(sink - mx)
            probs.append(jnp.concatenate(es, axis=0))
        acc = _dot(probs[0], keys_of(vs_scr, vs_hist, 0)) + _dot(probs[1], keys_of(vs_scr, vs_hist, 1))
        for i in range(2):
            c = 2 * g + i
            num = acc[Q_BLOCK * i:Q_BLOCK * (i + 1), 0:LANES]
            den = acc[Q_BLOCK * i:Q_BLOCK * (i + 1), LANES:] + jnp.where(half[0], sink_terms[i, 0], sink_terms[i, 1])
            a_scr[rows, LANES * c:LANES * (c + 1)] = num / den

    n_ffn = w_up_ref.shape[1] // FF_CHUNK
    n_conv, n_attn = tm // CONV_ROWS, (tm // Q_BLOCK) * N_KV_HEADS
    for unit in range(n_ffn):
        for i in range(unit * n_attn // n_ffn, (unit + 1) * n_attn // n_ffn):
            attention_unit(i)
        ffn_chunk(unit)
        for i in range(unit * n_conv // n_ffn, (unit + 1) * n_conv // n_ffn):
            conv_chunk(i)

    xf = xf + functools.reduce(lambda acc, part: acc + part, ffn_parts)
    gate = jax.nn.sigmoid(_dot(_rmsnorm(xf, g_ple_ref[...]).astype(BF16), w_gate_ref[...]))
    p_rows = jnp.where(t == 0, ps_ref[...], p_ref[...])
    xf = xf + _dot(p_rows.astype(BF16), w_ple_ref[...]) * gate
    if g_final is not None:
        xf = _rmsnorm(xf, g_final)
    y_ref[...] = xf

    mix = _post_mix(a_scr[...], c_scr[...], ln_g_ref[...], ln_b_ref[...], w_pw2_ref,
                    g_ao_ref[...], g_co_ref[...], w_o_ref)
    x1_scr[slot] = x + mix

    for g in range(N_KV_HEADS):
        for p in range(2):
            kt_hist[1 - slot, g, p] = kt_scr[g, p, tm - WINDOW:tm, :]
            vs_hist[1 - slot, g, p, :, 0:LANES] = vs_scr[g, p, tm - WINDOW:tm, 0:LANES]
    for c in range(512 // LANES):
        u_scr[c, pl.ds(0, CONV_PAD, stride=2), :] = u_scr[c, pl.ds(2 * tm, CONV_PAD, stride=2), :]

    @pl.when(t == 0)
    def _():
        ys_ref[...] = y_ref[...]


def _layer(layer, x, cos, sin, p, xs, ps, w, g_final=None):
    b, s, d = x.shape
    tm = xs.shape[0]
    n_s = s // tm
    n_tiles = b * n_s
    lay = lambda *_: (layer, 0, 0)
    vec = lambda n: _resident((None, 1, n), lay)
    mixer_tile = lambda t: jnp.minimum(t, n_tiles - 1)
    ffn_tile = lambda t: jnp.maximum(t - 1, 0)
    seq_of = lambda t: mixer_tile(t) // n_s
    in_specs = [
        pl.BlockSpec(memory_space=pltpu.SMEM),
        pl.BlockSpec((tm, d), lambda t: (mixer_tile(t), 0)),
        pl.BlockSpec((tm, LANES), lambda t: (lax.rem(mixer_tile(t), n_s), 0)),
        pl.BlockSpec((tm, LANES), lambda t: (lax.rem(mixer_tile(t), n_s), 0)),
        pl.BlockSpec((None, tm, p.shape[-1]), lambda t: (layer, ffn_tile(t), 0)),
        _resident((tm, d), lambda t: (0, 0)),
        _resident((None, tm, p.shape[-1]), lay),
        vec(d),
        _resident((None,) + w['w_in'].shape[1:], lay),
        _resident((None, CONV_K, 512), lay),
        vec(512), vec(512), vec(512),
        _resident((None, 512, 512), lay),
        vec(512), vec(512),
        _resident((None, d, d), lay),
        vec(d),
        _resident((None,) + w['w_up'].shape[1:], lay), _resident((None,) + w['w_down'].shape[1:], lay),
        vec(d),
        _resident((None, d, d), lay), _resident((None,) + w['w_ple'].shape[1:], lay),
    ]
    args = [w['sinks'], x.reshape(b * s, d), cos, sin, p, xs, ps, w['g_mix'], w['w_in'], w['w_dw'], w['b_dw'],
            w['ln_g'], w['ln_b'], w['w_pw2'], w['g_attn_out'], w['g_conv_out'], w['w_o'], w['g_ffn'], w['w_up'],
            w['w_down'], w['g_ple'], w['w_ple_gate'], w['w_ple']]
    if g_final is not None:
        args.append(g_final)
        in_specs.append(_resident((1, d), lambda t: (0, 0)))
    out_shape = [
        jax.ShapeDtypeStruct((b * s, d), F32),
        jax.ShapeDtypeStruct((tm, d), F32),
        jax.ShapeDtypeStruct((b, WINDOW, D_KV), F32),
        jax.ShapeDtypeStruct((b, WINDOW, D_KV), F32),
        jax.ShapeDtypeStruct((b, CONV_PAD, 512), F32),
    ]
    out_specs = [
        pl.BlockSpec((tm, d), lambda t: (ffn_tile(t), 0)),
        pl.BlockSpec((tm, d), lambda t: (0, 0)),
        pl.BlockSpec((1, WINDOW, D_KV), lambda t: (seq_of(t), 0, 0)),
        pl.BlockSpec((1, WINDOW, D_KV), lambda t: (seq_of(t), 0, 0)),
        pl.BlockSpec((1, CONV_PAD, 512), lambda t: (seq_of(t), 0, 0)),
    ]
    scratch = [
        pltpu.VMEM((2, tm, d), F32),
        pltpu.VMEM((tm, D_ATTN), BF16),
        pltpu.VMEM((N_KV_HEADS, 2, tm, LANES), BF16),
        pltpu.VMEM((N_KV_HEADS, 2, tm, 2 * LANES), BF16),
        pltpu.VMEM((2, N_KV_HEADS, 2, WINDOW, LANES), BF16),
        pltpu.VMEM((2, N_KV_HEADS, 2, WINDOW, 2 * LANES), BF16),
        pltpu.VMEM((512 // LANES, 2 * (tm + CONV_PAD), LANES), F32),
        pltpu.VMEM((tm, D_ATTN), F32),
        pltpu.VMEM((tm, 512), F32),
    ]
    return pl.pallas_call(
        functools.partial(_layer_kernel, layer, tm, n_s, n_tiles, g_final is not None),
        grid=(n_tiles + 1,),
        in_specs=in_specs, out_specs=out_specs, out_shape=out_shape, scratch_shapes=scratch,
        compiler_params=pltpu.CompilerParams(
            dimension_semantics=("arbitrary",), vmem_limit_bytes=VMEM_LIMIT),
        name=f"layer_l{layer}",
    )(*args)


def _sample_out_proj_kernel(x_ref, a_ref, cpre_ref, ln_g_ref, ln_b_ref, w_pw2_ref, g_ao_ref, g_co_ref, w_o_ref,
                            out_ref):
    out_ref[...] = x_ref[...] + _post_mix(a_ref[...], cpre_ref[...], ln_g_ref[...], ln_b_ref[...], w_pw2_ref,
                                          g_ao_ref[...], g_co_ref[...], w_o_ref)


def _sample_out_proj(layer, x, a, cpre, w):
    t, d = x.shape
    lay = lambda *_: (layer, 0, 0)
    vec = lambda n: _resident((None, 1, n), lay)
    row = lambda n: pl.BlockSpec((t, n), lambda i: (0, 0))
    return pl.pallas_call(
        _sample_out_proj_kernel,
        grid=(1,),
        in_specs=[row(d), row(512), row(512), vec(512), vec(512), _resident((None, 512, 512), lay),
                  vec(512), vec(512), _resident((None, d, d), lay)],
        out_specs=row(d), out_shape=jax.ShapeDtypeStruct((t, d), F32),
        compiler_params=pltpu.CompilerParams(
            dimension_semantics=("arbitrary",), vmem_limit_bytes=VMEM_LIMIT),
        name=f"sample_out_proj_l{layer}",
    )(x, a, cpre, w['ln_g'], w['ln_b'], w['w_pw2'], w['g_attn_out'], w['g_conv_out'], w['w_o'])


def _sample_in_proj_kernel(x_ref, cos_ref, sin_ref, g_mix_ref, w_in_ref, q_ref, k_ref, v_ref, u_ref):
    qs, k, v, u = _in_proj(x_ref[...], g_mix_ref[...], w_in_ref, cos_ref[...], sin_ref[...])
    for c, qc in enumerate(qs):
        q_ref[:, LANES * c:LANES * (c + 1)] = qc
    k_ref[...] = k
    v_ref[...] = v
    u_ref[...] = u


def _sample_in_proj(layer, x, cos, sin, w):
    t, d = x.shape
    lay = lambda *_: (layer, 0, 0)
    full = lambda n: pl.BlockSpec((t, n), lambda i: (0, 0))
    return pl.pallas_call(
        _sample_in_proj_kernel,
        grid=(1,),
        in_specs=[full(d), full(LANES), full(LANES), _resident((None, 1, d), lay),
                  _resident((None,) + w['w_in'].shape[1:], lay)],
        out_specs=[full(D_ATTN), full(D_KV), full(D_KV), full(512)],
        out_shape=[jax.ShapeDtypeStruct((t, n), F32) for n in (D_ATTN, D_KV, D_KV, 512)],
        compiler_params=pltpu.CompilerParams(
            dimension_semantics=("arbitrary",), vmem_limit_bytes=VMEM_LIMIT),
        name=f"sample_in_proj_l{layer}",
    )(x, cos, sin, w['g_mix'], w['w_in'])


def _sample_mixer_kernel(layer, n_tok,
                         sinks_ref, q_ref, kn_ref, vn_ref, un_ref, ckt_ref, cvt_ref, st_ref, w_dw_ref, b_dw_ref,
                         kw_hbm, vw_hbm, cs_hbm,
                         a_ref, cpre_ref, kw_ref, vw_ref, cs_ref, un_scr, cp_scr):
    del kw_hbm, vw_hbm, cs_hbm
    n_seq = ckt_ref.shape[0]
    rows = n_seq * n_tok
    shift = n_tok.bit_length() - 1
    kct = jnp.concatenate([ckt_ref[s] for s in range(n_seq)], axis=1).astype(BF16)
    vct = jnp.concatenate([cvt_ref[s] for s in range(n_seq)], axis=1).astype(BF16)
    r = lax.broadcasted_iota(jnp.int32, (rows, n_seq * WINDOW), 0)
    c = lax.broadcasted_iota(jnp.int32, (rows, n_seq * WINDOW), 1)
    mask_cache = ((c >> 7) == (r >> shift)) & ((c & (WINDOW - 1)) >= (r & (n_tok - 1)))
    r = lax.broadcasted_iota(jnp.int32, (rows, rows), 0)
    c = lax.broadcasted_iota(jnp.int32, (rows, rows), 1)
    mask_new = ((c >> shift) == (r >> shift)) & ((c & (n_tok - 1)) <= (r & (n_tok - 1)))
    q_blocks = [q_ref[:, LANES * i:LANES * (i + 1)] for i in range(D_ATTN // LANES)]
    kn, vn = kn_ref[...], vn_ref[...]
    segments = [(kct, vct, True, mask_cache), (kn.astype(BF16), vn.astype(BF16), False, mask_new)]
    a_blocks = _attention(q_blocks, segments, lambda h: sinks_ref[layer, h])
    for i, ab in enumerate(a_blocks):
        a_ref[:, LANES * i:LANES * (i + 1)] = ab

    pad = jnp.zeros((WINDOW - rows, D_KV), F32)
    lane = lax.broadcasted_iota(jnp.int32, (D_KV, WINDOW), 1)
    tail = lane >= WINDOW - n_tok
    for new, old_ref, out_ref in ((kn, ckt_ref, kw_ref), (vn, cvt_ref, vw_ref)):
        new_t = jnp.concatenate([new, pad], axis=0).T
        for s in range(n_seq):
            shifted = pltpu.roll(old_ref[s], WINDOW - n_tok, 1)
            placed = pltpu.roll(new_t, (WINDOW - n_tok - n_tok * s) % WINDOW, 1)
            out_ref[s] = jnp.where(tail, placed, shifted)

    for c in range(512 // LANES):
        un_scr[c] = un_ref[:, LANES * c:LANES * (c + 1)]
    new_planes = [jnp.concatenate([un_scr[c, pl.ds(j, n_seq, stride=n_tok), :] for c in range(512 // LANES)], axis=1)
                  for j in range(n_tok)]
    plane = lambda i: st_ref[i] if i < CONV_HIST else new_planes[i - CONV_HIST]
    for j in range(n_tok):
        acc = jnp.zeros((n_seq, 512), F32)
        for tap in range(CONV_K):
            acc = acc + plane(j + tap) * w_dw_ref[tap:tap + 1, :]
        acc = acc + b_dw_ref[...]
        for c in range(512 // LANES):
            cp_scr[c, pl.ds(j, n_seq, stride=n_tok), :] = acc[:, LANES * c:LANES * (c + 1)]
    for c in range(512 // LANES):
        cpre_ref[:, LANES * c:LANES * (c + 1)] = cp_scr[c]
    for i in range(CONV_HIST):
        cs_ref[i] = plane(i + n_tok)


def _sample_mixer(layer, n_tok, q, kn, vn, un, cache_kt, cache_vt, state_t, windows, w):
    n_b = cache_kt.shape[1]
    g = SEQS_PER_STEP
    rows = g * n_tok
    lay = lambda *_: (layer, 0, 0)
    row = lambda n: pl.BlockSpec((rows, n), lambda i: (i, 0))
    seq = pl.BlockSpec((None, g, D_KV, WINDOW), lambda i: (layer, i, 0, 0))
    planes = pl.BlockSpec((None, CONV_HIST, g, 512), lambda i: (layer, 0, i, 0))
    hbm = pl.BlockSpec(memory_space=pl.ANY)
    args = [w['sinks'], q, kn, vn, un, cache_kt, cache_vt, state_t, w['w_dw'], w['b_dw'], *windows]
    first_window = len(args) - len(windows)
    return pl.pallas_call(
        functools.partial(_sample_mixer_kernel, layer, n_tok),
        grid=(n_b // g,),
        in_specs=[pl.BlockSpec(memory_space=pltpu.SMEM), row(D_ATTN), row(D_KV), row(D_KV), row(512),
                  seq, seq, planes, _resident((None, CONV_K, 512), lay), _resident((None, 1, 512), lay),
                  hbm, hbm, hbm],
        out_specs=[row(D_ATTN), row(512), seq, seq, planes],
        out_shape=[jax.ShapeDtypeStruct((n_b * n_tok, D_ATTN), F32),
                   jax.ShapeDtypeStruct((n_b * n_tok, 512), F32)]
                  + [jax.ShapeDtypeStruct(b.shape, F32) for b in windows],
        input_output_aliases={first_window + i: 2 + i for i in range(len(windows))},
        scratch_shapes=[pltpu.VMEM((512 // LANES, rows, LANES), F32), pltpu.VMEM((512 // LANES, rows, LANES), F32)],
        compiler_params=pltpu.CompilerParams(
            dimension_semantics=("parallel",), vmem_limit_bytes=VMEM_LIMIT),
        name=f"sample_mixer_l{layer}",
    )(*args)


def _rope_tables(pos):
    half = HEAD_DIM // 2
    inv = 1.0 / (ROPE_THETA ** (jnp.arange(half, dtype=F32) / half))
    ang = pos.astype(F32)[:, None] * inv[None, :]
    cos, sin = jnp.cos(ang), jnp.sin(ang)
    return jnp.tile(cos, (1, 4)), jnp.tile(jnp.concatenate([-sin, sin], axis=1), (1, 2))


def kernel(x_prompt, x_sample, p_prompt, p_sample, cache_k, cache_v, state_conv, g_mix, w_in, sinks, w_dw, b_dw,
           ln_g, ln_b, w_pw2, g_attn_out, g_conv_out, w_o, g_ffn, w_up, w_down, g_ple, w_ple_gate, w_ple, g_final):
    depth = w_in.shape[0]
    bp, sp, d = x_prompt.shape
    bs, ts, _ = x_sample.shape
    vec = lambda a: a.reshape(depth, 1, a.shape[-1])
    w = dict(
        sinks=sinks, g_mix=vec(g_mix), w_in=w_in.astype(BF16), w_dw=w_dw, b_dw=vec(b_dw), ln_g=vec(ln_g),
        ln_b=vec(ln_b), w_pw2=w_pw2.astype(BF16), g_attn_out=vec(g_attn_out), g_conv_out=vec(g_conv_out),
        w_o=w_o.astype(BF16), g_ffn=vec(g_ffn), w_up=w_up.astype(BF16), w_down=w_down.astype(BF16),
        g_ple=vec(g_ple), w_ple_gate=w_ple_gate.astype(BF16), w_ple=w_ple.astype(BF16))
    g_fin = g_final.reshape(1, d)

    cos_p, sin_p = _rope_tables(jnp.arange(sp, dtype=jnp.int32))
    pos_s = PAST_LEN + (jnp.arange(bs * ts, dtype=jnp.int32) % ts)
    cos_s, sin_s = _rope_tables(pos_s)

    pp = p_prompt.reshape(depth, bp * sp, -1)
    ps = p_sample.reshape(depth, bs * ts, -1)
    feature_major = lambda c: c.transpose(0, 1, 3, 4, 2).reshape(depth, bs, D_KV, WINDOW)
    ckt, cvt = feature_major(cache_k), feature_major(cache_v)
    st = state_conv.transpose(0, 2, 1, 3)
    windows = (jnp.zeros_like(ckt), jnp.zeros_like(cvt), jnp.zeros_like(st))

    yp = x_prompt
    ys = x_sample.reshape(bs * ts, d)
    kp, vp, up = [], [], []
    for i in range(depth):
        last = g_fin if i == depth - 1 else None
        q, kn, vn, un = _sample_in_proj(i, ys, cos_s, sin_s, w)
        a, cpre, *windows = _sample_mixer(i, ts, q, kn, vn, un, ckt, cvt, st, windows, w)
        ys = _sample_out_proj(i, ys, a, cpre, w)

        yp, ys, kt, vt, ut = _layer(i, yp, cos_p, sin_p, pp, ys, ps, w, g_final=last)
        yp = yp.reshape(bp, sp, d)
        kp.append(kt); vp.append(vt); up.append(ut[:, CONV_PAD - CONV_HIST:, :])

    heads = lambda lst, b: jnp.stack(lst).reshape(depth, b, WINDOW, N_KV_HEADS, HEAD_DIM)
    window_major = lambda c: c.reshape(depth, bs, N_KV_HEADS, HEAD_DIM, WINDOW).transpose(0, 1, 4, 2, 3)
    kw, vw, cs = windows
    return (yp, ys.reshape(bs, ts, d), heads(kp, bp), heads(vp, bp), jnp.stack(up),
            window_major(kw), window_major(vw), cs.transpose(0, 2, 1, 3))
```

```python
import functools

import jax
import jax.numpy as jnp
from jax import lax
from jax.experimental import pallas as pl
from jax.experimental.pallas import tpu as pltpu

F32 = jnp.float32
BF16 = jnp.bfloat16

HEAD_DIM = 64
N_Q_HEADS = 8
N_KV_HEADS = 2
Q_PER_KV = N_Q_HEADS // N_KV_HEADS
D_ATTN = N_Q_HEADS * HEAD_DIM
D_KV = N_KV_HEADS * HEAD_DIM
WINDOW = 128
CONV_K = 31
CONV_HIST = CONV_K - 1
PAST_LEN = 8192
ROPE_THETA = 10000.0
EPS = 1e-6
NEG = -1e30

LANES = 128
Q_BLOCK = 128
CONV_PAD = 32
CONV_ROWS = 64
FF_CHUNK = 1024
SEQS_PER_STEP = 8
VMEM_LIMIT = 56 * 1024 * 1024


def _rmsnorm(x, g):
    ms = jnp.mean(x * x, axis=-1, keepdims=True)
    return x * lax.rsqrt(ms + EPS) * g


def _dot(a, b):
    return jnp.dot(a, b, preferred_element_type=F32)


def _swap_half_heads(x):
    lane = lax.broadcasted_iota(jnp.int32, x.shape, 1)
    first = (lane & 32) == 0
    return jnp.where(first, pltpu.roll(x, LANES - 32, 1), pltpu.roll(x, 32, 1))


def _rope(x, cos, sin):
    return x * cos + _swap_half_heads(x) * sin


def _in_proj(x, g_mix, w_in_ref, cos, sin):
    h = _rmsnorm(x, g_mix).astype(BF16)
    q = _dot(h, w_in_ref[:, 0:D_ATTN])
    kv = _dot(h, w_in_ref[:, D_ATTN:D_ATTN + 2 * D_KV])
    o = D_ATTN + 2 * D_KV
    ua = _dot(h, w_in_ref[:, o:o + 512])
    ug = _dot(h, w_in_ref[:, o + 512:o + 1024])
    scale = HEAD_DIM ** -0.5
    qs = [_rope(q[:, LANES * c:LANES * (c + 1)], cos, sin) * scale for c in range(D_ATTN // LANES)]
    k = _rope(kv[:, 0:D_KV], cos, sin)
    v = kv[:, D_KV:2 * D_KV]
    u = ua * jax.nn.sigmoid(ug)
    return qs, k, v, u


def _attention(q_blocks, segments, sink_of_head):
    nn = (((1,), (0,)), ((), ()))
    nt = (((1,), (1,)), ((), ()))
    rows = q_blocks[0].shape[0]
    lane = lax.broadcasted_iota(jnp.int32, (rows, LANES), 1)
    low = lane < HEAD_DIM
    q_hat = []
    for h in range(N_Q_HEADS):
        g = h // Q_PER_KV
        blk = q_blocks[h // 2]
        if (h % 2) != g:
            blk = pltpu.roll(blk, HEAD_DIM, 1)
        q_hat.append(jnp.where(low if g == 0 else ~low, blk, 0.0).astype(BF16))
    q_hat = jnp.concatenate(q_hat, axis=0)
    scores = [lax.dot_general(q_hat, k, nn if tr else nt, preferred_element_type=F32)
              for k, _, tr, _ in segments]
    probs = [[] for _ in segments]
    dens = []
    for h in range(N_Q_HEADS):
        sink = sink_of_head(h)
        masked = [jnp.where(m, s[h * rows:(h + 1) * rows], NEG) for s, (_, _, _, m) in zip(scores, segments)]
        mx = sink
        for sm in masked:
            mx = jnp.maximum(mx, jnp.max(sm, axis=-1, keepdims=True))
        den = jnp.exp(sink - mx)
        for i, sm in enumerate(masked):
            e = jnp.exp(sm - mx)
            den = den + jnp.sum(e, axis=-1, keepdims=True)
            probs[i].append(e.astype(BF16))
        dens.append(den)
    out = None
    for i, (_, v, tr, _) in enumerate(segments):
        o = lax.dot_general(jnp.concatenate(probs[i], axis=0), v, nt if tr else nn,
                            preferred_element_type=F32)
        out = o if out is None else out + o
    placed = []
    for h in range(N_Q_HEADS):
        g = h // Q_PER_KV
        o = out[h * rows:(h + 1) * rows] / dens[h]
        if (h % 2) != g:
            o = pltpu.roll(o, HEAD_DIM, 1)
        placed.append(o)
    return [jnp.where(low, placed[2 * c], placed[2 * c + 1]) for c in range(D_ATTN // LANES)]


def _post_mix(a, cpre, ln_g, ln_b, w_pw2_ref, g_ao, g_co, w_o_ref):
    mu = jnp.mean(cpre, axis=-1, keepdims=True)
    xc = cpre - mu
    var = jnp.mean(xc * xc, axis=-1, keepdims=True)
    y = xc * lax.rsqrt(var + EPS) * ln_g + ln_b
    y = y * jax.nn.sigmoid(y)
    c = _dot(y.astype(BF16), w_pw2_ref[...])
    an = _rmsnorm(a, g_ao).astype(BF16)
    cn = _rmsnorm(c, g_co).astype(BF16)
    return _dot(an, w_o_ref[0:D_ATTN, :]) + _dot(cn, w_o_ref[D_ATTN:, :])


def _resident(shape, index_map):
    return pl.BlockSpec(shape, index_map, pipeline_mode=pl.Buffered(1))


def _layer_kernel(layer, tm, n_s, n_tiles, final, *refs):
    (sinks_ref, x_ref, cos_ref, sin_ref, p_ref, xs_ref, ps_ref, g_mix_ref, w_in_ref, w_dw_ref, b_dw_ref,
     ln_g_ref, ln_b_ref, w_pw2_ref, g_ao_ref, g_co_ref, w_o_ref, g_ffn_ref, w_up_ref, w_down_ref, g_ple_ref,
     w_gate_ref, w_ple_ref) = refs[:23]
    refs = refs[23:]
    g_final = None
    if final:
        g_final = refs[0][...]
        refs = refs[1:]
    (y_ref, ys_ref, k_tail_ref, v_tail_ref, u_tail_ref,
     x1_scr, q_scr, kt_scr, vs_scr, kt_hist, vs_hist, u_scr, a_scr, c_scr) = refs

    t = pl.program_id(0)
    slot = t & 1
    s_idx = lax.rem(jnp.minimum(t, n_tiles - 1), n_s)
    lane = lax.broadcasted_iota(jnp.int32, (1, LANES), 1)
    half = [lane < HEAD_DIM, lane >= HEAD_DIM]
    ones_half = [jnp.where(half[p], 1.0, 0.0).astype(BF16) for p in range(2)]

    @pl.when(t == 0)
    def _():
        x1_scr[1] = xs_ref[...]
        for g in range(N_KV_HEADS):
            for p in range(2):
                vs_scr[g, p, :, LANES:] = jnp.broadcast_to(ones_half[p], (tm, LANES))
                for sl in range(2):
                    vs_hist[sl, g, p, :, LANES:] = jnp.broadcast_to(ones_half[p], (WINDOW, LANES))

    @pl.when(s_idx == 0)
    def _():
        for g in range(N_KV_HEADS):
            for p in range(2):
                kt_hist[slot, g, p] = jnp.zeros((WINDOW, LANES), BF16)
                vs_hist[slot, g, p, :, 0:LANES] = jnp.zeros((WINDOW, LANES), BF16)
        for c in range(512 // LANES):
            u_scr[c, pl.ds(0, CONV_PAD, stride=2), :] = jnp.zeros((CONV_PAD, LANES), F32)

    xf = x1_scr[1 - slot]
    xfn = _rmsnorm(xf, g_ffn_ref[...]).astype(BF16)

    x = x_ref[...]
    qs, k, v, u = _in_proj(x, g_mix_ref[...], w_in_ref, cos_ref[...], sin_ref[...])
    for c, qc in enumerate(qs):
        q_scr[:, LANES * c:LANES * (c + 1)] = qc.astype(BF16)
    k_swapped, v_swapped = pltpu.roll(k, HEAD_DIM, 1), pltpu.roll(v, HEAD_DIM, 1)
    for g in range(N_KV_HEADS):
        for p in range(2):
            kt_scr[g, p] = jnp.where(half[p], k if p == g else k_swapped, 0.0).astype(BF16)
            vs_scr[g, p, :, 0:LANES] = jnp.where(half[p], v if p == g else v_swapped, 0.0).astype(BF16)
    for c in range(512 // LANES):
        u_scr[c, pl.ds(2 * CONV_PAD, tm, stride=2), :] = u[:, LANES * c:LANES * (c + 1)]
    k_tail_ref[0] = k[tm - WINDOW:, :]
    v_tail_ref[0] = v[tm - WINDOW:, :]
    u_tail_ref[0] = u[tm - CONV_PAD:, :]

    r = lax.broadcasted_iota(jnp.int32, (Q_BLOCK, Q_BLOCK + WINDOW), 0)
    l = lax.broadcasted_iota(jnp.int32, (Q_BLOCK, Q_BLOCK + WINDOW), 1)
    band = (l >= r) & (l <= r + WINDOW)
    first_band = band & (l + (s_idx * tm - WINDOW) >= 0)
    nt = (((1,), (1,)), ((), ()))
    ffn_parts = []

    def ffn_chunk(unit):
        hc = _dot(xfn, w_up_ref[:, FF_CHUNK * unit:FF_CHUNK * (unit + 1)])
        hc = jnp.square(jnp.maximum(hc, 0.0)).astype(BF16)
        ffn_parts.append(_dot(hc, w_down_ref[FF_CHUNK * unit:FF_CHUNK * (unit + 1), :]))

    def conv_chunk(unit):
        r0 = CONV_ROWS * unit
        for c in range(512 // LANES):
            cols = slice(LANES * c, LANES * (c + 1))
            conv = jnp.zeros((CONV_ROWS, LANES), F32)
            for tap in range(CONV_K):
                lo = r0 + (CONV_PAD - CONV_HIST) + tap
                conv = conv + u_scr[c, pl.ds(2 * lo, CONV_ROWS, stride=2), :] * w_dw_ref[tap:tap + 1, cols]
            c_scr[r0:r0 + CONV_ROWS, cols] = conv + b_dw_ref[:, cols]

    def attention_unit(unit):
        j, g = divmod(unit, N_KV_HEADS)
        rows = slice(Q_BLOCK * j, Q_BLOCK * (j + 1))
        mask = first_band if j == 0 else band

        def keys_of(cur, hist, p):
            if j == 0:
                return jnp.concatenate([hist[slot, g, p], cur[g, p, 0:Q_BLOCK]], axis=0)
            return cur[g, p, Q_BLOCK * j - WINDOW:Q_BLOCK * (j + 1)]

        q_pair = jnp.concatenate([q_scr[rows, LANES * c:LANES * (c + 1)] for c in (2 * g, 2 * g + 1)], axis=0)
        probs, sink_terms = [], {}
        for p in range(2):
            s = lax.dot_general(q_pair, keys_of(kt_scr, kt_hist, p), nt, preferred_element_type=F32)
            es = []
            for i in range(2):
                sink = sinks_ref[layer, 4 * g + 2 * i + p]
                sm = jnp.where(mask, s[Q_BLOCK * i:Q_BLOCK * (i + 1)], NEG)
                mx = jnp.maximum(jnp.max(sm, axis=-1, keepdims=True), sink)
                es.append(jnp.exp(sm - mx).astype(BF16))
                sink_terms[i, p] = jnp.exp(sink - mx)
            probs.append(jnp.concatenate(es, axis=0))
        acc = _dot(probs[0], keys_of(vs_scr, vs_hist, 0)) + _dot(probs[1], keys_of(vs_scr, vs_hist, 1))
        for i in range(2):
            c = 2 * g + i
            num = acc[Q_BLOCK * i:Q_BLOCK * (i + 1), 0:LANES]
            den = acc[Q_BLOCK * i:Q_BLOCK * (i + 1), LANES:] + jnp.where(half[0], sink_terms[i, 0], sink_terms[i, 1])
            a_scr[rows, LANES * c:LANES * (c + 1)] = num / den

    n_ffn = w_up_ref.shape[1] // FF_CHUNK
    n_conv, n_attn = tm // CONV_ROWS, (tm // Q_BLOCK) * N_KV_HEADS
    for unit in range(n_ffn):
        for i in range(unit * n_attn // n_ffn, (unit + 1) * n_attn // n_ffn):
            attention_unit(i)
        ffn_chunk(unit)
        for i in range(unit * n_conv // n_ffn, (unit + 1) * n_conv // n_ffn):
            conv_chunk(i)

    xf = xf + functools.reduce(lambda acc, part: acc + part, ffn_parts)
    gate = jax.nn.sigmoid(_dot(_rmsnorm(xf, g_ple_ref[...]).astype(BF16), w_gate_ref[...]))
    p_rows = jnp.where(t == 0, ps_ref[...], p_ref[...])
    xf = xf + _dot(p_rows.astype(BF16), w_ple_ref[...]) * gate
    if g_final is not None:
        xf = _rmsnorm(xf, g_final)
    y_ref[...] = xf

    mix = _post_mix(a_scr[...], c_scr[...], ln_g_ref[...], ln_b_ref[...], w_pw2_ref,
                    g_ao_ref[...], g_co_ref[...], w_o_ref)
    x1_scr[slot] = x + mix

    for g in range(N_KV_HEADS):
        for p in range(2):
            kt_hist[1 - slot, g, p] = kt_scr[g, p, tm - WINDOW:tm, :]
            vs_hist[1 - slot, g, p, :, 0:LANES] = vs_scr[g, p, tm - WINDOW:tm, 0:LANES]
    for c in range(512 // LANES):
        u_scr[c, pl.ds(0, CONV_PAD, stride=2), :] = u_scr[c, pl.ds(2 * tm, CONV_PAD, stride=2), :]

    @pl.when(t == 0)
    def _():
        ys_ref[...] = y_ref[...]


def _layer(layer, x, cos, sin, p, xs, ps, w, g_final=None):
    b, s, d = x.shape
    tm = xs.shape[0]
    n_s = s // tm
    n_tiles = b * n_s
    lay = lambda *_: (layer, 0, 0)
    vec = lambda n: _resident((None, 1, n), lay)
    mixer_tile = lambda t: jnp.minimum(t, n_tiles - 1)
    ffn_tile = lambda t: jnp.maximum(t - 1, 0)
    seq_of = lambda t: mixer_tile(t) // n_s
    in_specs = [
        pl.BlockSpec(memory_space=pltpu.SMEM),
        pl.BlockSpec((tm, d), lambda t: (mixer_tile(t), 0)),
        pl.BlockSpec((tm, LANES), lambda t: (lax.rem(mixer_tile(t), n_s), 0)),
        pl.BlockSpec((tm, LANES), lambda t: (lax.rem(mixer_tile(t), n_s), 0)),
        pl.BlockSpec((None, tm, p.shape[-1]), lambda t: (layer, ffn_tile(t), 0)),
        _resident((tm, d), lambda t: (0, 0)),
        _resident((None, tm, p.shape[-1]), lay),
        vec(d),
        _resident((None,) + w['w_in'].shape[1:], lay),
        _resident((None, CONV_K, 512), lay),
        vec(512), vec(512), vec(512),
        _resident((None, 512, 512), lay),
        vec(512), vec(512),
        _resident((None, d, d), lay),
        vec(d),
        _resident((None,) + w['w_up'].shape[1:], lay), _resident((None,) + w['w_down'].shape[1:], lay),
        vec(d),
        _resident((None, d, d), lay), _resident((None,) + w['w_ple'].shape[1:], lay),
    ]
    args = [w['sinks'], x.reshape(b * s, d), cos, sin, p, xs, ps, w['g_mix'], w['w_in'], w['w_dw'], w['b_dw'],
            w['ln_g'], w['ln_b'], w['w_pw2'], w['g_attn_out'], w['g_conv_out'], w['w_o'], w['g_ffn'], w['w_up'],
            w['w_down'], w['g_ple'], w['w_ple_gate'], w['w_ple']]
    if g_final is not None:
        args.append(g_final)
        in_specs.append(_resident((1, d), lambda t: (0, 0)))
    out_shape = [
        jax.ShapeDtypeStruct((b * s, d), F32),
        jax.ShapeDtypeStruct((tm, d), F32),
        jax.ShapeDtypeStruct((b, WINDOW, D_KV), F32),
        jax.ShapeDtypeStruct((b, WINDOW, D_KV), F32),
        jax.ShapeDtypeStruct((b, CONV_PAD, 512), F32),
    ]
    out_specs = [
        pl.BlockSpec((tm, d), lambda t: (ffn_tile(t), 0)),
        pl.BlockSpec((tm, d), lambda t: (0, 0)),
        pl.BlockSpec((1, WINDOW, D_KV), lambda t: (seq_of(t), 0, 0)),
        pl.BlockSpec((1, WINDOW, D_KV), lambda t: (seq_of(t), 0, 0)),
        pl.BlockSpec((1, CONV_PAD, 512), lambda t: (seq_of(t), 0, 0)),
    ]
    scratch = [
        pltpu.VMEM((2, tm, d), F32),
        pltpu.VMEM((tm, D_ATTN), BF16),
        pltpu.VMEM((N_KV_HEADS, 2, tm, LANES), BF16),
        pltpu.VMEM((N_KV_HEADS, 2, tm, 2 * LANES), BF16),
        pltpu.VMEM((2, N_KV_HEADS, 2, WINDOW, LANES), BF16),
        pltpu.VMEM((2, N_KV_HEADS, 2, WINDOW, 2 * LANES), BF16),
        pltpu.VMEM((512 // LANES, 2 * (tm + CONV_PAD), LANES), F32),
        pltpu.VMEM((tm, D_ATTN), F32),
        pltpu.VMEM((tm, 512), F32),
    ]
    return pl.pallas_call(
        functools.partial(_layer_kernel, layer, tm, n_s, n_tiles, g_final is not None),
        grid=(n_tiles + 1,),
        in_specs=in_specs, out_specs=out_specs, out_shape=out_shape, scratch_shapes=scratch,
        compiler_params=pltpu.CompilerParams(
            dimension_semantics=("arbitrary",), vmem_limit_bytes=VMEM_LIMIT),
        name=f"layer_l{layer}",
    )(*args)


def _sample_mixer_kernel(layer, n_tok,
                         sinks_ref, x_ref, cos_ref, sin_ref, g_mix_ref, w_in_ref, ckt_ref, cvt_ref, st_ref, w_dw_ref,
                         b_dw_ref, ln_g_ref, ln_b_ref, w_pw2_ref, g_ao_ref, g_co_ref, w_o_ref, *refs):
    x1_ref, kw_ref, vw_ref, cs_ref, q_all, kn_all, vn_all, un_all, a_all, cp_all = refs[-10:]
    n_seq = ckt_ref.shape[0]
    rows = n_seq * n_tok
    shift = n_tok.bit_length() - 1
    step = pl.program_id(0)
    r0 = pl.multiple_of(step * rows, rows)

    @pl.when(step == 0)
    def _():
        qs, k, v, u = _in_proj(x_ref[...], g_mix_ref[...], w_in_ref, cos_ref[...], sin_ref[...])
        for c, qc in enumerate(qs):
            q_all[:, LANES * c:LANES * (c + 1)] = qc
        kn_all[...] = k
        vn_all[...] = v
        for c in range(512 // LANES):
            un_all[c] = u[:, LANES * c:LANES * (c + 1)]

    kct = jnp.concatenate([ckt_ref[s] for s in range(n_seq)], axis=1).astype(BF16)
    vct = jnp.concatenate([cvt_ref[s] for s in range(n_seq)], axis=1).astype(BF16)
    r = lax.broadcasted_iota(jnp.int32, (rows, n_seq * WINDOW), 0)
    c = lax.broadcasted_iota(jnp.int32, (rows, n_seq * WINDOW), 1)
    mask_cache = ((c >> 7) == (r >> shift)) & ((c & (WINDOW - 1)) >= (r & (n_tok - 1)))
    r = lax.broadcasted_iota(jnp.int32, (rows, rows), 0)
    c = lax.broadcasted_iota(jnp.int32, (rows, rows), 1)
    mask_new = ((c >> shift) == (r >> shift)) & ((c & (n_tok - 1)) <= (r & (n_tok - 1)))
    q_blocks = [q_all[pl.ds(r0, rows), LANES * i:LANES * (i + 1)] for i in range(D_ATTN // LANES)]
    kn, vn = kn_all[pl.ds(r0, rows), :], vn_all[pl.ds(r0, rows), :]
    segments = [(kct, vct, True, mask_cache), (kn.astype(BF16), vn.astype(BF16), False, mask_new)]
    a_blocks = _attention(q_blocks, segments, lambda h: sinks_ref[layer, h])
    for i, ab in enumerate(a_blocks):
        a_all[pl.ds(r0, rows), LANES * i:LANES * (i + 1)] = ab

    pad = jnp.zeros((WINDOW - rows, D_KV), F32)
    lane = lax.broadcasted_iota(jnp.int32, (D_KV, WINDOW), 1)
    tail = lane >= WINDOW - n_tok
    for new, old_ref, out_ref in ((kn, ckt_ref, kw_ref), (vn, cvt_ref, vw_ref)):
        new_t = jnp.concatenate([new, pad], axis=0).T
        for s in range(n_seq):
            shifted = pltpu.roll(old_ref[s], WINDOW - n_tok, 1)
            placed = pltpu.roll(new_t, (WINDOW - n_tok - n_tok * s) % WINDOW, 1)
            out_ref[s] = jnp.where(tail, placed, shifted)

    token_rows = lambda j: pl.ds(r0 + j, n_seq, stride=n_tok)
    new_planes = [jnp.concatenate([un_all[c, token_rows(j), :] for c in range(512 // LANES)], axis=1)
                  for j in range(n_tok)]
    plane = lambda i: st_ref[i] if i < CONV_HIST else new_planes[i - CONV_HIST]
    for j in range(n_tok):
        acc = jnp.zeros((n_seq, 512), F32)
        for tap in range(CONV_K):
            acc = acc + plane(j + tap) * w_dw_ref[tap:tap + 1, :]
        acc = acc + b_dw_ref[...]
        for c in range(512 // LANES):
            cp_all[c, token_rows(j), :] = acc[:, LANES * c:LANES * (c + 1)]
    for i in range(CONV_HIST):
        cs_ref[i] = plane(i + n_tok)

    @pl.when(step == pl.num_programs(0) - 1)
    def _():
        cpre = jnp.concatenate([cp_all[c] for c in range(512 // LANES)], axis=1)
        x1_ref[...] = x_ref[...] + _post_mix(a_all[...], cpre, ln_g_ref[...], ln_b_ref[...], w_pw2_ref,
                                             g_ao_ref[...], g_co_ref[...], w_o_ref)


def _sample_mixer(layer, n_tok, x, cos, sin, cache_kt, cache_vt, state_t, windows, w):
    t, d = x.shape
    n_b = cache_kt.shape[1]
    g = SEQS_PER_STEP
    lay = lambda *_: (layer, 0, 0)
    vec = lambda n: _resident((None, 1, n), lay)
    full = lambda n: _resident((t, n), lambda i: (0, 0))
    seq = pl.BlockSpec((None, g, D_KV, WINDOW), lambda i: (layer, i, 0, 0))
    planes = pl.BlockSpec((None, CONV_HIST, g, 512), lambda i: (layer, 0, i, 0))
    args = [w['sinks'], x, cos, sin, w['g_mix'], w['w_in'], cache_kt, cache_vt, state_t, w['w_dw'], w['b_dw'],
            w['ln_g'], w['ln_b'], w['w_pw2'], w['g_attn_out'], w['g_conv_out'], w['w_o'], *windows]
    first_window = len(args) - len(windows)
    slabs = pltpu.VMEM((512 // LANES, t, LANES), F32)
    return pl.pallas_call(
        functools.partial(_sample_mixer_kernel, layer, n_tok),
        grid=(n_b // g,),
        in_specs=[pl.BlockSpec(memory_space=pltpu.SMEM), full(d), full(LANES), full(LANES), vec(d),
                  _resident((None,) + w['w_in'].shape[1:], lay), seq, seq, planes,
                  _resident((None, CONV_K, 512), lay), vec(512), vec(512), vec(512),
                  _resident((None, 512, 512), lay), vec(512), vec(512), _resident((None, d, d), lay)]
                 + [pl.BlockSpec(memory_space=pl.ANY)] * len(windows),
        out_specs=[pl.BlockSpec((t, d), lambda i: (0, 0)), seq, seq, planes],
        out_shape=[jax.ShapeDtypeStruct((t, d), F32)]
                  + [jax.ShapeDtypeStruct(b.shape, F32) for b in (cache_kt, cache_vt, state_t)],
        input_output_aliases={first_window + i: 1 + i for i in range(len(windows))},
        scratch_shapes=[pltpu.VMEM((t, D_ATTN), F32), pltpu.VMEM((t, D_KV), F32), pltpu.VMEM((t, D_KV), F32), slabs,
                        pltpu.VMEM((t, D_ATTN), F32), slabs],
        compiler_params=pltpu.CompilerParams(
            dimension_semantics=("arbitrary",), vmem_limit_bytes=VMEM_LIMIT),
        name=f"sample_mixer_l{layer}",
    )(*args)


def _rope_tables(pos):
    half = HEAD_DIM // 2
    inv = 1.0 / (ROPE_THETA ** (jnp.arange(half, dtype=F32) / half))
    ang = pos.astype(F32)[:, None] * inv[None, :]
    cos, sin = jnp.cos(ang), jnp.sin(ang)
    return jnp.tile(cos, (1, 4)), jnp.tile(jnp.concatenate([-sin, sin], axis=1), (1, 2))


def kernel(x_prompt, x_sample, p_prompt, p_sample, cache_k, cache_v, state_conv, g_mix, w_in, sinks, w_dw, b_dw,
           ln_g, ln_b, w_pw2, g_attn_out, g_conv_out, w_o, g_ffn, w_up, w_down, g_ple, w_ple_gate, w_ple, g_final):
    depth = w_in.shape[0]
    bp, sp, d = x_prompt.shape
    bs, ts, _ = x_sample.shape
    vec = lambda a: a.reshape(depth, 1, a.shape[-1])
    w = dict(
        sinks=sinks, g_mix=vec(g_mix), w_in=w_in.astype(BF16), w_dw=w_dw, b_dw=vec(b_dw), ln_g=vec(ln_g),
        ln_b=vec(ln_b), w_pw2=w_pw2.astype(BF16), g_attn_out=vec(g_attn_out), g_conv_out=vec(g_conv_out),
        w_o=w_o.astype(BF16), g_ffn=vec(g_ffn), w_up=w_up.astype(BF16), w_down=w_down.astype(BF16),
        g_ple=vec(g_ple), w_ple_gate=w_ple_gate.astype(BF16), w_ple=w_ple.astype(BF16))
    g_fin = g_final.reshape(1, d)

    cos_p, sin_p = _rope_tables(jnp.arange(sp, dtype=jnp.int32))
    pos_s = PAST_LEN + (jnp.arange(bs * ts, dtype=jnp.int32) % ts)
    cos_s, sin_s = _rope_tables(pos_s)

    pp = p_prompt.reshape(depth, bp * sp, -1)
    ps = p_sample.reshape(depth, bs * ts, -1)
    feature_major = lambda c: c.transpose(0, 1, 3, 4, 2).reshape(depth, bs, D_KV, WINDOW)
    ckt, cvt = feature_major(cache_k), feature_major(cache_v)
    st = state_conv.transpose(0, 2, 1, 3)
    windows = ()

    yp = x_prompt
    ys = x_sample.reshape(bs * ts, d)
    kp, vp, up = [], [], []
    for i in range(depth):
        last = g_fin if i == depth - 1 else None
        ys, *windows = _sample_mixer(i, ts, ys, cos_s, sin_s, ckt, cvt, st, windows, w)

        yp, ys, kt, vt, ut = _layer(i, yp, cos_p, sin_p, pp, ys, ps, w, g_final=last)
        yp = yp.reshape(bp, sp, d)
        kp.append(kt); vp.append(vt); up.append(ut[:, CONV_PAD - CONV_HIST:, :])

    heads = lambda lst, b: jnp.stack(lst).reshape(depth, b, WINDOW, N_KV_HEADS, HEAD_DIM)
    window_major = lambda c: c.reshape(depth, bs, N_KV_HEADS, HEAD_DIM, WINDOW).transpose(0, 1, 4, 2, 3)
    kw, vw, cs = windows
    return (yp, ys.reshape(bs, ts, d), heads(kp, bp), heads(vp, bp), jnp.stack(up),
            window_major(kw), window_major(vw), cs.transpose(0, 2, 1, 3))
```

```python
import functools

import jax
import jax.numpy as jnp
from jax import lax
from jax.experimental import pallas as pl
from jax.experimental.pallas import tpu as pltpu

F32 = jnp.float32
BF16 = jnp.bfloat16

HEAD_DIM = 64
N_Q_HEADS = 8
N_KV_HEADS = 2
Q_PER_KV = N_Q_HEADS // N_KV_HEADS
D_ATTN = N_Q_HEADS * HEAD_DIM
D_KV = N_KV_HEADS * HEAD_DIM
WINDOW = 128
CONV_K = 31
CONV_HIST = CONV_K - 1
PAST_LEN = 8192
ROPE_THETA = 10000.0
EPS = 1e-6
NEG = -1e30

LANES = 128
Q_BLOCK = 128
CONV_PAD = 32
CONV_ROWS = 64
FF_CHUNK = 1024
SEQS_PER_STEP = 8
VMEM_LIMIT = 56 * 1024 * 1024


def _rmsnorm(x, g):
    ms = jnp.mean(x * x, axis=-1, keepdims=True)
    return x * lax.rsqrt(ms + EPS) * g


def _dot(a, b):
    return jnp.dot(a, b, preferred_element_type=F32)


def _swap_half_heads(x):
    lane = lax.broadcasted_iota(jnp.int32, x.shape, 1)
    first = (lane & 32) == 0
    return jnp.where(first, pltpu.roll(x, LANES - 32, 1), pltpu.roll(x, 32, 1))


def _rope(x, cos, sin):
    return x * cos + _swap_half_heads(x) * sin


def _in_proj(x, g_mix, w_in_ref, cos, sin):
    h = _rmsnorm(x, g_mix).astype(BF16)
    q = _dot(h, w_in_ref[:, 0:D_ATTN])
    kv = _dot(h, w_in_ref[:, D_ATTN:D_ATTN + 2 * D_KV])
    o = D_ATTN + 2 * D_KV
    ua = _dot(h, w_in_ref[:, o:o + 512])
    ug = _dot(h, w_in_ref[:, o + 512:o + 1024])
    scale = HEAD_DIM ** -0.5
    qs = [_rope(q[:, LANES * c:LANES * (c + 1)], cos, sin) * scale for c in range(D_ATTN // LANES)]
    k = _rope(kv[:, 0:D_KV], cos, sin)
    v = kv[:, D_KV:2 * D_KV]
    u = ua * jax.nn.sigmoid(ug)
    return qs, k, v, u


def _attention(q_blocks, segments, sink_of_head):
    nn = (((1,), (0,)), ((), ()))
    nt = (((1,), (1,)), ((), ()))
    rows = q_blocks[0].shape[0]
    lane = lax.broadcasted_iota(jnp.int32, (rows, LANES), 1)
    low = lane < HEAD_DIM
    q_hat = []
    for h in range(N_Q_HEADS):
        g = h // Q_PER_KV
        blk = q_blocks[h // 2]
        if (h % 2) != g:
            blk = pltpu.roll(blk, HEAD_DIM, 1)
        q_hat.append(jnp.where(low if g == 0 else ~low, blk, 0.0).astype(BF16))
    q_hat = jnp.concatenate(q_hat, axis=0)
    scores = [lax.dot_general(q_hat, k, nn if tr else nt, preferred_element_type=F32)
              for k, _, tr, _ in segments]
    probs = [[] for _ in segments]
    dens = []
    for h in range(N_Q_HEADS):
        sink = sink_of_head(h)
        masked = [jnp.where(m, s[h * rows:(h + 1) * rows], NEG) for s, (_, _, _, m) in zip(scores, segments)]
        mx = sink
        for sm in masked:
            mx = jnp.maximum(mx, jnp.max(sm, axis=-1, keepdims=True))
        den = jnp.exp(sink - mx)
        for i, sm in enumerate(masked):
            e = jnp.exp(sm - mx)
            den = den + jnp.sum(e, axis=-1, keepdims=True)
            probs[i].append(e.astype(BF16))
        dens.append(den)
    out = None
    for i, (_, v, tr, _) in enumerate(segments):
        o = lax.dot_general(jnp.concatenate(probs[i], axis=0), v, nt if tr else nn,
                            preferred_element_type=F32)
        out = o if out is None else out + o
    placed = []
    for h in range(N_Q_HEADS):
        g = h // Q_PER_KV
        o = out[h * rows:(h + 1) * rows] / dens[h]
        if (h % 2) != g:
            o = pltpu.roll(o, HEAD_DIM, 1)
        placed.append(o)
    return [jnp.where(low, placed[2 * c], placed[2 * c + 1]) for c in range(D_ATTN // LANES)]


def _post_mix(a, cpre, ln_g, ln_b, w_pw2_ref, g_ao, g_co, w_o_ref):
    mu = jnp.mean(cpre, axis=-1, keepdims=True)
    xc = cpre - mu
    var = jnp.mean(xc * xc, axis=-1, keepdims=True)
    y = xc * lax.rsqrt(var + EPS) * ln_g + ln_b
    y = y * jax.nn.sigmoid(y)
    c = _dot(y.astype(BF16), w_pw2_ref[...])
    an = _rmsnorm(a, g_ao).astype(BF16)
    cn = _rmsnorm(c, g_co).astype(BF16)
    return _dot(an, w_o_ref[0:D_ATTN, :]) + _dot(cn, w_o_ref[D_ATTN:, :])


def _resident(shape, index_map):
    return pl.BlockSpec(shape, index_map, pipeline_mode=pl.Buffered(1))


def _layer_kernel(layer, tm, n_s, n_tiles, final, *refs):
    (sinks_ref, x_ref, cos_ref, sin_ref, p_ref, xs_ref, ps_ref, g_mix_ref, w_in_ref, w_dw_ref, b_dw_ref,
     ln_g_ref, ln_b_ref, w_pw2_ref, g_ao_ref, g_co_ref, w_o_ref, g_ffn_ref, w_up_ref, w_down_ref, g_ple_ref,
     w_gate_ref, w_ple_ref) = refs[:23]
    refs = refs[23:]
    g_final = None
    if final:
        g_final = refs[0][...]
        refs = refs[1:]
    (y_ref, ys_ref, k_tail_ref, v_tail_ref, u_tail_ref,
     x1_scr, q_scr, kt_scr, vs_scr, kt_hist, vs_hist, u_scr, a_scr, c_scr) = refs

    t = pl.program_id(0)
    slot = t & 1
    s_idx = lax.rem(jnp.minimum(t, n_tiles - 1), n_s)
    lane = lax.broadcasted_iota(jnp.int32, (1, LANES), 1)
    half = [lane < HEAD_DIM, lane >= HEAD_DIM]
    ones_half = [jnp.where(half[p], 1.0, 0.0).astype(BF16) for p in range(2)]

    @pl.when(t == 0)
    def _():
        x1_scr[1] = xs_ref[...]
        for g in range(N_KV_HEADS):
            for p in range(2):
                vs_scr[g, p, :, LANES:] = jnp.broadcast_to(ones_half[p], (tm, LANES))
                for sl in range(2):
                    vs_hist[sl, g, p, :, LANES:] = jnp.broadcast_to(ones_half[p], (WINDOW, LANES))

    @pl.when(s_idx == 0)
    def _():
        for g in range(N_KV_HEADS):
            for p in range(2):
                kt_hist[slot, g, p] = jnp.zeros((WINDOW, LANES), BF16)
                vs_hist[slot, g, p, :, 0:LANES] = jnp.zeros((WINDOW, LANES), BF16)
        for c in range(512 // LANES):
            u_scr[c, pl.ds(0, CONV_PAD, stride=2), :] = jnp.zeros((CONV_PAD, LANES), F32)

    xf = x1_scr[1 - slot]
    xfn = _rmsnorm(xf, g_ffn_ref[...]).astype(BF16)

    x = x_ref[...]
    qs, k, v, u = _in_proj(x, g_mix_ref[...], w_in_ref, cos_ref[...], sin_ref[...])
    for c, qc in enumerate(qs):
        q_scr[:, LANES * c:LANES * (c + 1)] = qc.astype(BF16)
    k_swapped, v_swapped = pltpu.roll(k, HEAD_DIM, 1), pltpu.roll(v, HEAD_DIM, 1)
    for g in range(N_KV_HEADS):
        for p in range(2):
            kt_scr[g, p] = jnp.where(half[p], k if p == g else k_swapped, 0.0).astype(BF16)
            vs_scr[g, p, :, 0:LANES] = jnp.where(half[p], v if p == g else v_swapped, 0.0).astype(BF16)
    for c in range(512 // LANES):
        u_scr[c, pl.ds(2 * CONV_PAD, tm, stride=2), :] = u[:, LANES * c:LANES * (c + 1)]
    k_tail_ref[0] = k[tm - WINDOW:, :]
    v_tail_ref[0] = v[tm - WINDOW:, :]
    u_tail_ref[0] = u[tm - CONV_PAD:, :]

    r = lax.broadcasted_iota(jnp.int32, (Q_BLOCK, Q_BLOCK + WINDOW), 0)
    l = lax.broadcasted_iota(jnp.int32, (Q_BLOCK, Q_BLOCK + WINDOW), 1)
    band = (l >= r) & (l <= r + WINDOW)
    first_band = band & (l + (s_idx * tm - WINDOW) >= 0)
    nt = (((1,), (1,)), ((), ()))
    ffn_parts = []

    def ffn_chunk(unit):
        hc = _dot(xfn, w_up_ref[:, FF_CHUNK * unit:FF_CHUNK * (unit + 1)])
        hc = jnp.square(jnp.maximum(hc, 0.0)).astype(BF16)
        ffn_parts.append(_dot(hc, w_down_ref[FF_CHUNK * unit:FF_CHUNK * (unit + 1), :]))

    def conv_chunk(unit):
        r0 = CONV_ROWS * unit
        for c in range(512 // LANES):
            cols = slice(LANES * c, LANES * (c + 1))
            conv = jnp.zeros((CONV_ROWS, LANES), F32)
            for tap in range(CONV_K):
                lo = r0 + (CONV_PAD - CONV_HIST) + tap
                conv = conv + u_scr[c, pl.ds(2 * lo, CONV_ROWS, stride=2), :] * w_dw_ref[tap:tap + 1, cols]
            c_scr[r0:r0 + CONV_ROWS, cols] = conv + b_dw_ref[:, cols]

    def attention_unit(unit):
        j, g = divmod(unit, N_KV_HEADS)
        rows = slice(Q_BLOCK * j, Q_BLOCK * (j + 1))
        mask = first_band if j == 0 else band

        def keys_of(cur, hist, p):
            if j == 0:
                return jnp.concatenate([hist[slot, g, p], cur[g, p, 0:Q_BLOCK]], axis=0)
            return cur[g, p, Q_BLOCK * j - WINDOW:Q_BLOCK * (j + 1)]

        q_pair = jnp.concatenate([q_scr[rows, LANES * c:LANES * (c + 1)] for c in (2 * g, 2 * g + 1)], axis=0)
        probs, sink_terms = [], {}
        for p in range(2):
            s = lax.dot_general(q_pair, keys_of(kt_scr, kt_hist, p), nt, preferred_element_type=F32)
            es = []
            for i in range(2):
                sink = sinks_ref[layer, 4 * g + 2 * i + p]
                sm = jnp.where(mask, s[Q_BLOCK * i:Q_BLOCK * (i + 1)], NEG)
                mx = jnp.maximum(jnp.max(sm, axis=-1, keepdims=True), sink)
                es.append(jnp.exp(sm - mx).astype(BF16))
                sink_terms[i, p] = jnp.exp(sink - mx)
            probs.append(jnp.concatenate(es, axis=0))
        acc = _dot(probs[0], keys_of(vs_scr, vs_hist, 0)) + _dot(probs[1], keys_of(vs_scr, vs_hist, 1))
        for i in range(2):
            c = 2 * g + i
            num = acc[Q_BLOCK * i:Q_BLOCK * (i + 1), 0:LANES]
            den = acc[Q_BLOCK * i:Q_BLOCK * (i + 1), LANES:] + jnp.where(half[0], sink_terms[i, 0], sink_terms[i, 1])
            a_scr[rows, LANES * c:LANES * (c + 1)] = num / den

    n_ffn = w_up_ref.shape[1] // FF_CHUNK
    n_conv, n_attn = tm // CONV_ROWS, (tm // Q_BLOCK) * N_KV_HEADS
    for unit in range(n_ffn):
        for i in range(unit * n_attn // n_ffn, (unit + 1) * n_attn // n_ffn):
            attention_unit(i)
        ffn_chunk(unit)
        for i in range(unit * n_conv // n_ffn, (unit + 1) * n_conv // n_ffn):
            conv_chunk(i)

    xf = xf + functools.reduce(lambda acc, part: acc + part, ffn_parts)
    gate = jax.nn.sigmoid(_dot(_rmsnorm(xf, g_ple_ref[...]).astype(BF16), w_gate_ref[...]))
    p_rows = jnp.where(t == 0, ps_ref[...], p_ref[...])
    xf = xf + _dot(p_rows.astype(BF16), w_ple_ref[...]) * gate
    if g_final is not None:
        xf = _rmsnorm(xf, g_final)
    y_ref[...] = xf

    mix = _post_mix(a_scr[...], c_scr[...], ln_g_ref[...], ln_b_ref[...], w_pw2_ref,
                    g_ao_ref[...], g_co_ref[...], w_o_ref)
    x1_scr[slot] = x + mix

    for g in range(N_KV_HEADS):
        for p in range(2):
            kt_hist[1 - slot, g, p] = kt_scr[g, p, tm - WINDOW:tm, :]
            vs_hist[1 - slot, g, p, :, 0:LANES] = vs_scr[g, p, tm - WINDOW:tm, 0:LANES]
    for c in range(512 // LANES):
        u_scr[c, pl.ds(0, CONV_PAD, stride=2), :] = u_scr[c, pl.ds(2 * tm, CONV_PAD, stride=2), :]

    @pl.when(t == 0)
    def _():
        ys_ref[...] = y_ref[...]


def _layer(layer, x, cos, sin, p, xs, ps, w, w_up, w_down, g_final=None):
    b, s, d = x.shape
    tm = xs.shape[0]
    n_s = s // tm
    n_tiles = b * n_s
    lay = lambda *_: (layer, 0, 0)
    vec = lambda n: _resident((None, 1, n), lay)
    mixer_tile = lambda t: jnp.minimum(t, n_tiles - 1)
    ffn_tile = lambda t: jnp.maximum(t - 1, 0)
    seq_of = lambda t: mixer_tile(t) // n_s
    in_specs = [
        pl.BlockSpec(memory_space=pltpu.SMEM),
        pl.BlockSpec((tm, d), lambda t: (mixer_tile(t), 0)),
        pl.BlockSpec((tm, LANES), lambda t: (lax.rem(mixer_tile(t), n_s), 0)),
        pl.BlockSpec((tm, LANES), lambda t: (lax.rem(mixer_tile(t), n_s), 0)),
        pl.BlockSpec((None, tm, p.shape[-1]), lambda t: (layer, ffn_tile(t), 0)),
        _resident((tm, d), lambda t: (0, 0)),
        _resident((None, tm, p.shape[-1]), lay),
        vec(d),
        _resident((None,) + w['w_in'].shape[1:], lay),
        _resident((None, CONV_K, 512), lay),
        vec(512), vec(512), vec(512),
        _resident((None, 512, 512), lay),
        vec(512), vec(512),
        _resident((None, d, d), lay),
        vec(d),
        _resident(w_up.shape, lambda t: (0, 0)), _resident(w_down.shape, lambda t: (0, 0)),
        vec(d),
        _resident((None, d, d), lay), _resident((None,) + w['w_ple'].shape[1:], lay),
    ]
    args = [w['sinks'], x.reshape(b * s, d), cos, sin, p, xs, ps, w['g_mix'], w['w_in'], w['w_dw'], w['b_dw'],
            w['ln_g'], w['ln_b'], w['w_pw2'], w['g_attn_out'], w['g_conv_out'], w['w_o'], w['g_ffn'], w_up,
            w_down, w['g_ple'], w['w_ple_gate'], w['w_ple']]
    if g_final is not None:
        args.append(g_final)
        in_specs.append(_resident((1, d), lambda t: (0, 0)))
    out_shape = [
        jax.ShapeDtypeStruct((b * s, d), F32),
        jax.ShapeDtypeStruct((tm, d), F32),
        jax.ShapeDtypeStruct((b, WINDOW, D_KV), F32),
        jax.ShapeDtypeStruct((b, WINDOW, D_KV), F32),
        jax.ShapeDtypeStruct((b, CONV_PAD, 512), F32),
    ]
    out_specs = [
        pl.BlockSpec((tm, d), lambda t: (ffn_tile(t), 0)),
        pl.BlockSpec((tm, d), lambda t: (0, 0)),
        pl.BlockSpec((1, WINDOW, D_KV), lambda t: (seq_of(t), 0, 0)),
        pl.BlockSpec((1, WINDOW, D_KV), lambda t: (seq_of(t), 0, 0)),
        pl.BlockSpec((1, CONV_PAD, 512), lambda t: (seq_of(t), 0, 0)),
    ]
    scratch = [
        pltpu.VMEM((2, tm, d), F32),
        pltpu.VMEM((tm, D_ATTN), BF16),
        pltpu.VMEM((N_KV_HEADS, 2, tm, LANES), BF16),
        pltpu.VMEM((N_KV_HEADS, 2, tm, 2 * LANES), BF16),
        pltpu.VMEM((2, N_KV_HEADS, 2, WINDOW, LANES), BF16),
        pltpu.VMEM((2, N_KV_HEADS, 2, WINDOW, 2 * LANES), BF16),
        pltpu.VMEM((512 // LANES, 2 * (tm + CONV_PAD), LANES), F32),
        pltpu.VMEM((tm, D_ATTN), F32),
        pltpu.VMEM((tm, 512), F32),
    ]
    return pl.pallas_call(
        functools.partial(_layer_kernel, layer, tm, n_s, n_tiles, g_final is not None),
        grid=(n_tiles + 1,),
        in_specs=in_specs, out_specs=out_specs, out_shape=out_shape, scratch_shapes=scratch,
        compiler_params=pltpu.CompilerParams(
            dimension_semantics=("arbitrary",), vmem_limit_bytes=VMEM_LIMIT),
        name=f"layer_l{layer}",
    )(*args)


def _sample_mixer_kernel(layer, n_tok,
                         sinks_ref, x_ref, cos_ref, sin_ref, g_mix_ref, w_in_ref, ckt_ref, cvt_ref, st_ref, w_dw_ref,
                         b_dw_ref, ln_g_ref, ln_b_ref, w_pw2_ref, g_ao_ref, g_co_ref, w_o_ref, w_up_ref, w_down_ref,
                         *refs):
    (x1_ref, kw_ref, vw_ref, cs_ref, w_up_out, w_down_out,
     q_all, kn_all, vn_all, un_all, a_all, cp_all) = refs[-12:]
    w_up_out[...] = w_up_ref[...].astype(BF16)
    w_down_out[...] = w_down_ref[...].astype(BF16)
    n_seq = ckt_ref.shape[0]
    rows = n_seq * n_tok
    shift = n_tok.bit_length() - 1
    step = pl.program_id(0)
    r0 = pl.multiple_of(step * rows, rows)

    @pl.when(step == 0)
    def _():
        qs, k, v, u = _in_proj(x_ref[...], g_mix_ref[...], w_in_ref, cos_ref[...], sin_ref[...])
        for c, qc in enumerate(qs):
            q_all[:, LANES * c:LANES * (c + 1)] = qc
        kn_all[...] = k
        vn_all[...] = v
        for c in range(512 // LANES):
            un_all[c] = u[:, LANES * c:LANES * (c + 1)]

    kct = jnp.concatenate([ckt_ref[s] for s in range(n_seq)], axis=1).astype(BF16)
    vct = jnp.concatenate([cvt_ref[s] for s in range(n_seq)], axis=1).astype(BF16)
    r = lax.broadcasted_iota(jnp.int32, (rows, n_seq * WINDOW), 0)
    c = lax.broadcasted_iota(jnp.int32, (rows, n_seq * WINDOW), 1)
    mask_cache = ((c >> 7) == (r >> shift)) & ((c & (WINDOW - 1)) >= (r & (n_tok - 1)))
    r = lax.broadcasted_iota(jnp.int32, (rows, rows), 0)
    c = lax.broadcasted_iota(jnp.int32, (rows, rows), 1)
    mask_new = ((c >> shift) == (r >> shift)) & ((c & (n_tok - 1)) <= (r & (n_tok - 1)))
    q_blocks = [q_all[pl.ds(r0, rows), LANES * i:LANES * (i + 1)] for i in range(D_ATTN // LANES)]
    kn, vn = kn_all[pl.ds(r0, rows), :], vn_all[pl.ds(r0, rows), :]
    segments = [(kct, vct, True, mask_cache), (kn.astype(BF16), vn.astype(BF16), False, mask_new)]
    a_blocks = _attention(q_blocks, segments, lambda h: sinks_ref[layer, h])
    for i, ab in enumerate(a_blocks):
        a_all[pl.ds(r0, rows), LANES * i:LANES * (i + 1)] = ab

    pad = jnp.zeros((WINDOW - rows, D_KV), F32)
    lane = lax.broadcasted_iota(jnp.int32, (D_KV, WINDOW), 1)
    tail = lane >= WINDOW - n_tok
    for new, old_ref, out_ref in ((kn, ckt_ref, kw_ref), (vn, cvt_ref, vw_ref)):
        new_t = jnp.concatenate([new, pad], axis=0).T
        for s in range(n_seq):
            shifted = pltpu.roll(old_ref[s], WINDOW - n_tok, 1)
            placed = pltpu.roll(new_t, (WINDOW - n_tok - n_tok * s) % WINDOW, 1)
            out_ref[s] = jnp.where(tail, placed, shifted)

    token_rows = lambda j: pl.ds(r0 + j, n_seq, stride=n_tok)
    new_planes = [jnp.concatenate([un_all[c, token_rows(j), :] for c in range(512 // LANES)], axis=1)
                  for j in range(n_tok)]
    plane = lambda i: st_ref[i] if i < CONV_HIST else new_planes[i - CONV_HIST]
    for j in range(n_tok):
        acc = jnp.zeros((n_seq, 512), F32)
        for tap in range(CONV_K):
            acc = acc + plane(j + tap) * w_dw_ref[tap:tap + 1, :]
        acc = acc + b_dw_ref[...]
        for c in range(512 // LANES):
            cp_all[c, token_rows(j), :] = acc[:, LANES * c:LANES * (c + 1)]
    for i in range(CONV_HIST):
        cs_ref[i] = plane(i + n_tok)

    @pl.when(step == pl.num_programs(0) - 1)
    def _():
        cpre = jnp.concatenate([cp_all[c] for c in range(512 // LANES)], axis=1)
        x1_ref[...] = x_ref[...] + _post_mix(a_all[...], cpre, ln_g_ref[...], ln_b_ref[...], w_pw2_ref,
                                             g_ao_ref[...], g_co_ref[...], w_o_ref)


def _sample_mixer(layer, n_tok, x, cos, sin, cache_kt, cache_vt, state_t, windows, w):
    t, d = x.shape
    n_b = cache_kt.shape[1]
    g = SEQS_PER_STEP
    lay = lambda *_: (layer, 0, 0)
    vec = lambda n: _resident((None, 1, n), lay)
    full = lambda n: _resident((t, n), lambda i: (0, 0))
    seq = pl.BlockSpec((None, g, D_KV, WINDOW), lambda i: (layer, i, 0, 0))
    planes = pl.BlockSpec((None, CONV_HIST, g, 512), lambda i: (layer, 0, i, 0))
    steps = n_b // g
    w_up, w_down = w['w_up_f32'], w['w_down_f32']
    up_rows, down_rows = w_up.shape[1] // steps, w_down.shape[1] // steps
    args = [w['sinks'], x, cos, sin, w['g_mix'], w['w_in'], cache_kt, cache_vt, state_t, w['w_dw'], w['b_dw'],
            w['ln_g'], w['ln_b'], w['w_pw2'], w['g_attn_out'], w['g_conv_out'], w['w_o'], w_up, w_down, *windows]
    first_window = len(args) - len(windows)
    slabs = pltpu.VMEM((512 // LANES, t, LANES), F32)
    return pl.pallas_call(
        functools.partial(_sample_mixer_kernel, layer, n_tok),
        grid=(steps,),
        in_specs=[pl.BlockSpec(memory_space=pltpu.SMEM), full(d), full(LANES), full(LANES), vec(d),
                  _resident((None,) + w['w_in'].shape[1:], lay), seq, seq, planes,
                  _resident((None, CONV_K, 512), lay), vec(512), vec(512), vec(512),
                  _resident((None, 512, 512), lay), vec(512), vec(512), _resident((None, d, d), lay),
                  pl.BlockSpec((None, up_rows, w_up.shape[2]), lambda i: (layer, i, 0)),
                  pl.BlockSpec((None, down_rows, w_down.shape[2]), lambda i: (layer, i, 0))]
                 + [pl.BlockSpec(memory_space=pl.ANY)] * len(windows),
        out_specs=[pl.BlockSpec((t, d), lambda i: (0, 0)), seq, seq, planes,
                   pl.BlockSpec((up_rows, w_up.shape[2]), lambda i: (i, 0)),
                   pl.BlockSpec((down_rows, w_down.shape[2]), lambda i: (i, 0))],
        out_shape=[jax.ShapeDtypeStruct((t, d), F32)]
                  + [jax.ShapeDtypeStruct(b.shape, F32) for b in (cache_kt, cache_vt, state_t)]
                  + [jax.ShapeDtypeStruct(w_up.shape[1:], BF16), jax.ShapeDtypeStruct(w_down.shape[1:], BF16)],
        input_output_aliases={first_window + i: 1 + i for i in range(len(windows))},
        scratch_shapes=[pltpu.VMEM((t, D_ATTN), F32), pltpu.VMEM((t, D_KV), F32), pltpu.VMEM((t, D_KV), F32), slabs,
                        pltpu.VMEM((t, D_ATTN), F32), slabs],
        compiler_params=pltpu.CompilerParams(
            dimension_semantics=("arbitrary",), vmem_limit_bytes=VMEM_LIMIT),
        name=f"sample_mixer_l{layer}",
    )(*args)


def _rope_tables(pos):
    half = HEAD_DIM // 2
    inv = 1.0 / (ROPE_THETA ** (jnp.arange(half, dtype=F32) / half))
    ang = pos.astype(F32)[:, None] * inv[None, :]
    cos, sin = jnp.cos(ang), jnp.sin(ang)
    return jnp.tile(cos, (1, 4)), jnp.tile(jnp.concatenate([-sin, sin], axis=1), (1, 2))


def kernel(x_prompt, x_sample, p_prompt, p_sample, cache_k, cache_v, state_conv, g_mix, w_in, sinks, w_dw, b_dw,
           ln_g, ln_b, w_pw2, g_attn_out, g_conv_out, w_o, g_ffn, w_up, w_down, g_ple, w_ple_gate, w_ple, g_final):
    depth = w_in.shape[0]
    bp, sp, d = x_prompt.shape
    bs, ts, _ = x_sample.shape
    vec = lambda a: a.reshape(depth, 1, a.shape[-1])
    w = dict(
        sinks=sinks, g_mix=vec(g_mix), w_in=w_in.astype(BF16), w_dw=w_dw, b_dw=vec(b_dw), ln_g=vec(ln_g),
        ln_b=vec(ln_b), w_pw2=w_pw2.astype(BF16), g_attn_out=vec(g_attn_out), g_conv_out=vec(g_conv_out),
        w_o=w_o.astype(BF16), g_ffn=vec(g_ffn), w_up_f32=w_up, w_down_f32=w_down,
        g_ple=vec(g_ple), w_ple_gate=w_ple_gate.astype(BF16), w_ple=w_ple.astype(BF16))
    g_fin = g_final.reshape(1, d)

    cos_p, sin_p = _rope_tables(jnp.arange(sp, dtype=jnp.int32))
    pos_s = PAST_LEN + (jnp.arange(bs * ts, dtype=jnp.int32) % ts)
    cos_s, sin_s = _rope_tables(pos_s)

    pp = p_prompt.reshape(depth, bp * sp, -1)
    ps = p_sample.reshape(depth, bs * ts, -1)
    feature_major = lambda c: c.transpose(0, 1, 3, 4, 2).reshape(depth, bs, D_KV, WINDOW)
    ckt, cvt = feature_major(cache_k), feature_major(cache_v)
    st = state_conv.transpose(0, 2, 1, 3)
    windows = ()

    yp = x_prompt
    ys = x_sample.reshape(bs * ts, d)
    kp, vp, up = [], [], []
    for i in range(depth):
        last = g_fin if i == depth - 1 else None
        ys, *windows, w_up_i, w_down_i = _sample_mixer(i, ts, ys, cos_s, sin_s, ckt, cvt, st, windows, w)

        yp, ys, kt, vt, ut = _layer(i, yp, cos_p, sin_p, pp, ys, ps, w, w_up_i, w_down_i, g_final=last)
        yp = yp.reshape(bp, sp, d)
        kp.append(kt); vp.append(vt); up.append(ut[:, CONV_PAD - CONV_HIST:, :])

    heads = lambda lst, b: jnp.stack(lst).reshape(depth, b, WINDOW, N_KV_HEADS, HEAD_DIM)
    window_major = lambda c: c.reshape(depth, bs, N_KV_HEADS, HEAD_DIM, WINDOW).transpose(0, 1, 4, 2, 3)
    kw, vw, cs = windows
    return (yp, ys.reshape(bs, ts, d), heads(kp, bp), heads(vp, bp), jnp.stack(up),
            window_major(kw), window_major(vw), cs.transpose(0, 2, 1, 3))
```

```python
import functools

import jax
import jax.numpy as jnp
from jax import lax
from jax.experimental import pallas as pl
from jax.experimental.pallas import tpu as pltpu

F32 = jnp.float32
BF16 = jnp.bfloat16

HEAD_DIM = 64
N_Q_HEADS = 8
N_KV_HEADS = 2
Q_PER_KV = N_Q_HEADS // N_KV_HEADS
D_ATTN = N_Q_HEADS * HEAD_DIM
D_KV = N_KV_HEADS * HEAD_DIM
WINDOW = 128
CONV_K = 31
CONV_HIST = CONV_K - 1
PAST_LEN = 8192
ROPE_THETA = 10000.0
EPS = 1e-6
NEG = -1e30

LANES = 128
Q_BLOCK = 128
CONV_PAD = 32
CONV_ROWS = 64
FF_CHUNK = 1024
SEQS_PER_STEP = 8
VMEM_LIMIT = 56 * 1024 * 1024


def _rmsnorm(x, g):
    ms = jnp.mean(x * x, axis=-1, keepdims=True)
    return x * lax.rsqrt(ms + EPS) * g


def _dot(a, b):
    return jnp.dot(a, b, preferred_element_type=F32)


def _swap_half_heads(x):
    lane = lax.broadcasted_iota(jnp.int32, x.shape, 1)
    first = (lane & 32) == 0
    return jnp.where(first, pltpu.roll(x, LANES - 32, 1), pltpu.roll(x, 32, 1))


def _rope(x, cos, sin):
    return x * cos + _swap_half_heads(x) * sin


def _in_proj(x, g_mix, w_in_ref, cos, sin):
    h = _rmsnorm(x, g_mix).astype(BF16)
    q = _dot(h, w_in_ref[:, 0:D_ATTN])
    kv = _dot(h, w_in_ref[:, D_ATTN:D_ATTN + 2 * D_KV])
    o = D_ATTN + 2 * D_KV
    ua = _dot(h, w_in_ref[:, o:o + 512])
    ug = _dot(h, w_in_ref[:, o + 512:o + 1024])
    scale = HEAD_DIM ** -0.5
    qs = [_rope(q[:, LANES * c:LANES * (c + 1)], cos, sin) * scale for c in range(D_ATTN // LANES)]
    k = _rope(kv[:, 0:D_KV], cos, sin)
    v = kv[:, D_KV:2 * D_KV]
    u = ua * jax.nn.sigmoid(ug)
    return qs, k, v, u


def _attention(q_blocks, segments, sink_of_head):
    nn = (((1,), (0,)), ((), ()))
    nt = (((1,), (1,)), ((), ()))
    rows = q_blocks[0].shape[0]
    lane = lax.broadcasted_iota(jnp.int32, (rows, LANES), 1)
    low = lane < HEAD_DIM
    q_hat = []
    for h in range(N_Q_HEADS):
        g = h // Q_PER_KV
        blk = q_blocks[h // 2]
        if (h % 2) != g:
            blk = pltpu.roll(blk, HEAD_DIM, 1)
        q_hat.append(jnp.where(low if g == 0 else ~low, blk, 0.0).astype(BF16))
    q_hat = jnp.concatenate(q_hat, axis=0)
    scores = [lax.dot_general(q_hat, k, nn if tr else nt, preferred_element_type=F32)
              for k, _, tr, _ in segments]
    probs = [[] for _ in segments]
    dens = []
    for h in range(N_Q_HEADS):
        sink = sink_of_head(h)
        masked = [jnp.where(m, s[h * rows:(h + 1) * rows], NEG) for s, (_, _, _, m) in zip(scores, segments)]
        mx = sink
        for sm in masked:
            mx = jnp.maximum(mx, jnp.max(sm, axis=-1, keepdims=True))
        den = jnp.exp(sink - mx)
        for i, sm in enumerate(masked):
            e = jnp.exp(sm - mx)
            den = den + jnp.sum(e, axis=-1, keepdims=True)
            probs[i].append(e.astype(BF16))
        dens.append(den)
    out = None
    for i, (_, v, tr, _) in enumerate(segments):
        o = lax.dot_general(jnp.concatenate(probs[i], axis=0), v, nt if tr else nn,
                            preferred_element_type=F32)
        out = o if out is None else out + o
    placed = []
    for h in range(N_Q_HEADS):
        g = h // Q_PER_KV
        o = out[h * rows:(h + 1) * rows] / dens[h]
        if (h % 2) != g:
            o = pltpu.roll(o, HEAD_DIM, 1)
        placed.append(o)
    return [jnp.where(low, placed[2 * c], placed[2 * c + 1]) for c in range(D_ATTN // LANES)]


def _post_mix(a, cpre, ln_g, ln_b, w_pw2_ref, g_ao, g_co, w_o_ref):
    mu = jnp.mean(cpre, axis=-1, keepdims=True)
    xc = cpre - mu
    var = jnp.mean(xc * xc, axis=-1, keepdims=True)
    y = xc * lax.rsqrt(var + EPS) * ln_g + ln_b
    y = y * jax.nn.sigmoid(y)
    c = _dot(y.astype(BF16), w_pw2_ref[...])
    an = _rmsnorm(a, g_ao).astype(BF16)
    cn = _rmsnorm(c, g_co).astype(BF16)
    return _dot(an, w_o_ref[0:D_ATTN, :]) + _dot(cn, w_o_ref[D_ATTN:, :])


def _resident(shape, index_map):
    return pl.BlockSpec(shape, index_map, pipeline_mode=pl.Buffered(1))


def _layer_kernel(layer, tm, n_s, n_tiles, final, *refs):
    (sinks_ref, x_ref, cos_ref, sin_ref, p_ref, xs_ref, ps_ref, g_mix_ref, w_in_ref, w_dw_ref, b_dw_ref,
     ln_g_ref, ln_b_ref, w_pw2_ref, g_ao_ref, g_co_ref, w_o_ref, g_ffn_ref, w_up_ref, w_down_ref, g_ple_ref,
     w_gate_ref, w_ple_ref) = refs[:23]
    refs = refs[23:]
    g_final = None
    if final:
        g_final = refs[0][...]
        refs = refs[1:]
    (y_ref, ys_ref, k_tail_ref, v_tail_ref, u_tail_ref,
     x1_scr, q_scr, kt_scr, vs_scr, kt_hist, vs_hist, u_scr, a_scr, c_scr) = refs

    t = pl.program_id(0)
    slot = t & 1
    s_idx = lax.rem(jnp.minimum(t, n_tiles - 1), n_s)
    lane = lax.broadcasted_iota(jnp.int32, (1, LANES), 1)
    half = [lane < HEAD_DIM, lane >= HEAD_DIM]
    ones_half = [jnp.where(half[p], 1.0, 0.0).astype(BF16) for p in range(2)]

    @pl.when(t == 0)
    def _():
        x1_scr[1] = xs_ref[...]
        for g in range(N_KV_HEADS):
            for p in range(2):
                vs_scr[g, p, :, LANES:] = jnp.broadcast_to(ones_half[p], (tm, LANES))
                for sl in range(2):
                    vs_hist[sl, g, p, :, LANES:] = jnp.broadcast_to(ones_half[p], (WINDOW, LANES))

    @pl.when(s_idx == 0)
    def _():
        for g in range(N_KV_HEADS):
            for p in range(2):
                kt_hist[slot, g, p] = jnp.zeros((WINDOW, LANES), BF16)
                vs_hist[slot, g, p, :, 0:LANES] = jnp.zeros((WINDOW, LANES), BF16)
        for c in range(512 // LANES):
            u_scr[c, pl.ds(0, CONV_PAD, stride=2), :] = jnp.zeros((CONV_PAD, LANES), F32)

    xf = x1_scr[1 - slot]
    xfn = _rmsnorm(xf, g_ffn_ref[...]).astype(BF16)

    x = x_ref[...]
    qs, k, v, u = _in_proj(x, g_mix_ref[...], w_in_ref, cos_ref[...], sin_ref[...])
    for c, qc in enumerate(qs):
        q_scr[:, LANES * c:LANES * (c + 1)] = qc.astype(BF16)
    k_swapped, v_swapped = pltpu.roll(k, HEAD_DIM, 1), pltpu.roll(v, HEAD_DIM, 1)
    for g in range(N_KV_HEADS):
        for p in range(2):
            kt_scr[g, p] = jnp.where(half[p], k if p == g else k_swapped, 0.0).astype(BF16)
            vs_scr[g, p, :, 0:LANES] = jnp.where(half[p], v if p == g else v_swapped, 0.0).astype(BF16)
    for c in range(512 // LANES):
        u_scr[c, pl.ds(2 * CONV_PAD, tm, stride=2), :] = u[:, LANES * c:LANES * (c + 1)]
    k_tail_ref[0] = k[tm - WINDOW:, :]
    v_tail_ref[0] = v[tm - WINDOW:, :]
    u_tail_ref[0] = u[tm - CONV_PAD:, :]

    r = lax.broadcasted_iota(jnp.int32, (Q_BLOCK, Q_BLOCK + WINDOW), 0)
    l = lax.broadcasted_iota(jnp.int32, (Q_BLOCK, Q_BLOCK + WINDOW), 1)
    band = (l >= r) & (l <= r + WINDOW)
    first_band = band & (l + (s_idx * tm - WINDOW) >= 0)
    nt = (((1,), (1,)), ((), ()))
    ffn_parts = []

    def ffn_chunk(unit):
        hc = _dot(xfn, w_up_ref[:, FF_CHUNK * unit:FF_CHUNK * (unit + 1)])
        hc = jnp.square(jnp.maximum(hc, 0.0)).astype(BF16)
        ffn_parts.append(_dot(hc, w_down_ref[FF_CHUNK * unit:FF_CHUNK * (unit + 1), :]))

    def conv_chunk(unit):
        r0 = CONV_ROWS * unit
        for c in range(512 // LANES):
            cols = slice(LANES * c, LANES * (c + 1))
            conv = jnp.zeros((CONV_ROWS, LANES), F32)
            for tap in range(CONV_K):
                lo = r0 + (CONV_PAD - CONV_HIST) + tap
                conv = conv + u_scr[c, pl.ds(2 * lo, CONV_ROWS, stride=2), :] * w_dw_ref[tap:tap + 1, cols]
            c_scr[r0:r0 + CONV_ROWS, cols] = conv + b_dw_ref[:, cols]

    def attention_unit(unit):
        j, g = divmod(unit, N_KV_HEADS)
        rows = slice(Q_BLOCK * j, Q_BLOCK * (j + 1))
        mask = first_band if j == 0 else band

        def keys_of(cur, hist, p):
            if j == 0:
                return jnp.concatenate([hist[slot, g, p], cur[g, p, 0:Q_BLOCK]], axis=0)
            return cur[g, p, Q_BLOCK * j - WINDOW:Q_BLOCK * (j + 1)]

        q_pair = jnp.concatenate([q_scr[rows, LANES * c:LANES * (c + 1)] for c in (2 * g, 2 * g + 1)], axis=0)
        probs, sink_terms = [], {}
        for p in range(2):
            s = lax.dot_general(q_pair, keys_of(kt_scr, kt_hist, p), nt, preferred_element_type=F32)
            es = []
            for i in range(2):
                sink = sinks_ref[layer, 4 * g + 2 * i + p]
                sm = jnp.where(mask, s[Q_BLOCK * i:Q_BLOCK * (i + 1)], NEG)
                mx = jnp.maximum(jnp.max(sm, axis=-1, keepdims=True), sink)
                es.append(jnp.exp(sm - mx).astype(BF16))
                sink_terms[i, p] = jnp.exp(sink - mx)
            probs.append(jnp.concatenate(es, axis=0))
        acc = _dot(probs[0], keys_of(vs_scr, vs_hist, 0)) + _dot(probs[1], keys_of(vs_scr, vs_hist, 1))
        for i in range(2):
            c = 2 * g + i
            num = acc[Q_BLOCK * i:Q_BLOCK * (i + 1), 0:LANES]
            den = acc[Q_BLOCK * i:Q_BLOCK * (i + 1), LANES:] + jnp.where(half[0], sink_terms[i, 0], sink_terms[i, 1])
            a_scr[rows, LANES * c:LANES * (c + 1)] = num / den

    n_ffn = w_up_ref.shape[1] // FF_CHUNK
    n_conv, n_attn = tm // CONV_ROWS, (tm // Q_BLOCK) * N_KV_HEADS
    for unit in range(n_ffn):
        for i in range(unit * n_attn // n_ffn, (unit + 1) * n_attn // n_ffn):
            attention_unit(i)
        ffn_chunk(unit)
        for i in range(unit * n_conv // n_ffn, (unit + 1) * n_conv // n_ffn):
            conv_chunk(i)

    xf = xf + functools.reduce(lambda acc, part: acc + part, ffn_parts)
    gate = jax.nn.sigmoid(_dot(_rmsnorm(xf, g_ple_ref[...]).astype(BF16), w_gate_ref[...]))
    p_rows = jnp.where(t == 0, ps_ref[...], p_ref[...])
    xf = xf + _dot(p_rows.astype(BF16), w_ple_ref[...]) * gate
    if g_final is not None:
        xf = _rmsnorm(xf, g_final)
    y_ref[...] = xf

    mix = _post_mix(a_scr[...], c_scr[...], ln_g_ref[...], ln_b_ref[...], w_pw2_ref,
                    g_ao_ref[...], g_co_ref[...], w_o_ref)
    x1_scr[slot] = x + mix

    for g in range(N_KV_HEADS):
        for p in range(2):
            kt_hist[1 - slot, g, p] = kt_scr[g, p, tm - WINDOW:tm, :]
            vs_hist[1 - slot, g, p, :, 0:LANES] = vs_scr[g, p, tm - WINDOW:tm, 0:LANES]
    for c in range(512 // LANES):
        u_scr[c, pl.ds(0, CONV_PAD, stride=2), :] = u_scr[c, pl.ds(2 * tm, CONV_PAD, stride=2), :]

    @pl.when(t == 0)
    def _():
        ys_ref[...] = y_ref[...]


def _layer(layer, x, cos, sin, p, xs, ps, w, w_up, w_down, w_gate, g_final=None):
    b, s, d = x.shape
    tm = xs.shape[0]
    n_s = s // tm
    n_tiles = b * n_s
    lay = lambda *_: (layer, 0, 0)
    vec = lambda n: _resident((None, 1, n), lay)
    mixer_tile = lambda t: jnp.minimum(t, n_tiles - 1)
    ffn_tile = lambda t: jnp.maximum(t - 1, 0)
    seq_of = lambda t: mixer_tile(t) // n_s
    in_specs = [
        pl.BlockSpec(memory_space=pltpu.SMEM),
        pl.BlockSpec((tm, d), lambda t: (mixer_tile(t), 0)),
        pl.BlockSpec((tm, LANES), lambda t: (lax.rem(mixer_tile(t), n_s), 0)),
        pl.BlockSpec((tm, LANES), lambda t: (lax.rem(mixer_tile(t), n_s), 0)),
        pl.BlockSpec((None, tm, p.shape[-1]), lambda t: (layer, ffn_tile(t), 0)),
        _resident((tm, d), lambda t: (0, 0)),
        _resident((None, tm, p.shape[-1]), lay),
        vec(d),
        _resident((None,) + w['w_in'].shape[1:], lay),
        _resident((None, CONV_K, 512), lay),
        vec(512), vec(512), vec(512),
        _resident((None, 512, 512), lay),
        vec(512), vec(512),
        _resident((None, d, d), lay),
        vec(d),
        _resident(w_up.shape, lambda t: (0, 0)), _resident(w_down.shape, lambda t: (0, 0)),
        vec(d),
        _resident(w_gate.shape, lambda t: (0, 0)), _resident((None,) + w['w_ple'].shape[1:], lay),
    ]
    args = [w['sinks'], x.reshape(b * s, d), cos, sin, p, xs, ps, w['g_mix'], w['w_in'], w['w_dw'], w['b_dw'],
            w['ln_g'], w['ln_b'], w['w_pw2'], w['g_attn_out'], w['g_conv_out'], w['w_o'], w['g_ffn'], w_up,
            w_down, w['g_ple'], w_gate, w['w_ple']]
    if g_final is not None:
        args.append(g_final)
        in_specs.append(_resident((1, d), lambda t: (0, 0)))
    out_shape = [
        jax.ShapeDtypeStruct((b * s, d), F32),
        jax.ShapeDtypeStruct((tm, d), F32),
        jax.ShapeDtypeStruct((b, WINDOW, D_KV), F32),
        jax.ShapeDtypeStruct((b, WINDOW, D_KV), F32),
        jax.ShapeDtypeStruct((b, CONV_PAD, 512), F32),
    ]
    out_specs = [
        pl.BlockSpec((tm, d), lambda t: (ffn_tile(t), 0)),
        pl.BlockSpec((tm, d), lambda t: (0, 0)),
        pl.BlockSpec((1, WINDOW, D_KV), lambda t: (seq_of(t), 0, 0)),
        pl.BlockSpec((1, WINDOW, D_KV), lambda t: (seq_of(t), 0, 0)),
        pl.BlockSpec((1, CONV_PAD, 512), lambda t: (seq_of(t), 0, 0)),
    ]
    scratch = [
        pltpu.VMEM((2, tm, d), F32),
        pltpu.VMEM((tm, D_ATTN), BF16),
        pltpu.VMEM((N_KV_HEADS, 2, tm, LANES), BF16),
        pltpu.VMEM((N_KV_HEADS, 2, tm, 2 * LANES), BF16),
        pltpu.VMEM((2, N_KV_HEADS, 2, WINDOW, LANES), BF16),
        pltpu.VMEM((2, N_KV_HEADS, 2, WINDOW, 2 * LANES), BF16),
        pltpu.VMEM((512 // LANES, 2 * (tm + CONV_PAD), LANES), F32),
        pltpu.VMEM((tm, D_ATTN), F32),
        pltpu.VMEM((tm, 512), F32),
    ]
    return pl.pallas_call(
        functools.partial(_layer_kernel, layer, tm, n_s, n_tiles, g_final is not None),
        grid=(n_tiles + 1,),
        in_specs=in_specs, out_specs=out_specs, out_shape=out_shape, scratch_shapes=scratch,
        compiler_params=pltpu.CompilerParams(
            dimension_semantics=("arbitrary",), vmem_limit_bytes=VMEM_LIMIT),
        name=f"layer_l{layer}",
    )(*args)


def _sample_mixer_kernel(layer, n_tok,
                         sinks_ref, x_ref, cos_ref, sin_ref, g_mix_ref, w_in_ref, ckt_ref, cvt_ref, st_ref, w_dw_ref,
                         b_dw_ref, ln_g_ref, ln_b_ref, w_pw2_ref, g_ao_ref, g_co_ref, w_o_ref, w_up_ref, w_down_ref,
                         w_gate_ref, *refs):
    (x1_ref, kw_ref, vw_ref, cs_ref, w_up_out, w_down_out, w_gate_out,
     q_all, kn_all, vn_all, un_all, a_all, cp_all) = refs[-13:]
    w_up_out[...] = w_up_ref[...].astype(BF16)
    w_down_out[...] = w_down_ref[...].astype(BF16)
    w_gate_out[...] = w_gate_ref[...].astype(BF16)
    n_seq = ckt_ref.shape[0]
    rows = n_seq * n_tok
    shift = n_tok.bit_length() - 1
    step = pl.program_id(0)
    r0 = pl.multiple_of(step * rows, rows)

    @pl.when(step == 0)
    def _():
        qs, k, v, u = _in_proj(x_ref[...], g_mix_ref[...], w_in_ref, cos_ref[...], sin_ref[...])
        for c, qc in enumerate(qs):
            q_all[:, LANES * c:LANES * (c + 1)] = qc
        kn_all[...] = k
        vn_all[...] = v
        for c in range(512 // LANES):
            un_all[c] = u[:, LANES * c:LANES * (c + 1)]

    kct = jnp.concatenate([ckt_ref[s] for s in range(n_seq)], axis=1).astype(BF16)
    vct = jnp.concatenate([cvt_ref[s] for s in range(n_seq)], axis=1).astype(BF16)
    r = lax.broadcasted_iota(jnp.int32, (rows, n_seq * WINDOW), 0)
    c = lax.broadcasted_iota(jnp.int32, (rows, n_seq * WINDOW), 1)
    mask_cache = ((c >> 7) == (r >> shift)) & ((c & (WINDOW - 1)) >= (r & (n_tok - 1)))
    r = lax.broadcasted_iota(jnp.int32, (rows, rows), 0)
    c = lax.broadcasted_iota(jnp.int32, (rows, rows), 1)
    mask_new = ((c >> shift) == (r >> shift)) & ((c & (n_tok - 1)) <= (r & (n_tok - 1)))
    q_blocks = [q_all[pl.ds(r0, rows), LANES * i:LANES * (i + 1)] for i in range(D_ATTN // LANES)]
    kn, vn = kn_all[pl.ds(r0, rows), :], vn_all[pl.ds(r0, rows), :]
    segments = [(kct, vct, True, mask_cache), (kn.astype(BF16), vn.astype(BF16), False, mask_new)]
    a_blocks = _attention(q_blocks, segments, lambda h: sinks_ref[layer, h])
    for i, ab in enumerate(a_blocks):
        a_all[pl.ds(r0, rows), LANES * i:LANES * (i + 1)] = ab

    pad = jnp.zeros((WINDOW - rows, D_KV), F32)
    lane = lax.broadcasted_iota(jnp.int32, (D_KV, WINDOW), 1)
    tail = lane >= WINDOW - n_tok
    for new, old_ref, out_ref in ((kn, ckt_ref, kw_ref), (vn, cvt_ref, vw_ref)):
        new_t = jnp.concatenate([new, pad], axis=0).T
        for s in range(n_seq):
            shifted = pltpu.roll(old_ref[s], WINDOW - n_tok, 1)
            placed = pltpu.roll(new_t, (WINDOW - n_tok - n_tok * s) % WINDOW, 1)
            out_ref[s] = jnp.where(tail, placed, shifted)

    token_rows = lambda j: pl.ds(r0 + j, n_seq, stride=n_tok)
    new_planes = [jnp.concatenate([un_all[c, token_rows(j), :] for c in range(512 // LANES)], axis=1)
                  for j in range(n_tok)]
    plane = lambda i: st_ref[i] if i < CONV_HIST else new_planes[i - CONV_HIST]
    for j in range(n_tok):
        acc = jnp.zeros((n_seq, 512), F32)
        for tap in range(CONV_K):
            acc = acc + plane(j + tap) * w_dw_ref[tap:tap + 1, :]
        acc = acc + b_dw_ref[...]
        for c in range(512 // LANES):
            cp_all[c, token_rows(j), :] = acc[:, LANES * c:LANES * (c + 1)]
    for i in range(CONV_HIST):
        cs_ref[i] = plane(i + n_tok)

    @pl.when(step == pl.num_programs(0) - 1)
    def _():
        cpre = jnp.concatenate([cp_all[c] for c in range(512 // LANES)], axis=1)
        x1_ref[...] = x_ref[...] + _post_mix(a_all[...], cpre, ln_g_ref[...], ln_b_ref[...], w_pw2_ref,
                                             g_ao_ref[...], g_co_ref[...], w_o_ref)


def _sample_mixer(layer, n_tok, x, cos, sin, cache_kt, cache_vt, state_t, windows, w):
    t, d = x.shape
    n_b = cache_kt.shape[1]
    g = SEQS_PER_STEP
    lay = lambda *_: (layer, 0, 0)
    vec = lambda n: _resident((None, 1, n), lay)
    full = lambda n: _resident((t, n), lambda i: (0, 0))
    seq = pl.BlockSpec((None, g, D_KV, WINDOW), lambda i: (layer, i, 0, 0))
    planes = pl.BlockSpec((None, CONV_HIST, g, 512), lambda i: (layer, 0, i, 0))
    steps = n_b // g
    w_up, w_down, w_gate = w['w_up_f32'], w['w_down_f32'], w['w_gate_f32']
    up_rows, down_rows = w_up.shape[1] // steps, w_down.shape[1] // steps
    gate_rows = w_gate.shape[1] // steps
    args = [w['sinks'], x, cos, sin, w['g_mix'], w['w_in'], cache_kt, cache_vt, state_t, w['w_dw'], w['b_dw'],
            w['ln_g'], w['ln_b'], w['w_pw2'], w['g_attn_out'], w['g_conv_out'], w['w_o'], w_up, w_down, w_gate,
            *windows]
    first_window = len(args) - len(windows)
    slabs = pltpu.VMEM((512 // LANES, t, LANES), F32)
    return pl.pallas_call(
        functools.partial(_sample_mixer_kernel, layer, n_tok),
        grid=(steps,),
        in_specs=[pl.BlockSpec(memory_space=pltpu.SMEM), full(d), full(LANES), full(LANES), vec(d),
                  _resident((None,) + w['w_in'].shape[1:], lay), seq, seq, planes,
                  _resident((None, CONV_K, 512), lay), vec(512), vec(512), vec(512),
                  _resident((None, 512, 512), lay), vec(512), vec(512), _resident((None, d, d), lay),
                  pl.BlockSpec((None, up_rows, w_up.shape[2]), lambda i: (layer, i, 0)),
                  pl.BlockSpec((None, down_rows, w_down.shape[2]), lambda i: (layer, i, 0)),
                  pl.BlockSpec((None, gate_rows, w_gate.shape[2]), lambda i: (layer, i, 0))]
                 + [pl.BlockSpec(memory_space=pl.ANY)] * len(windows),
        out_specs=[pl.BlockSpec((t, d), lambda i: (0, 0)), seq, seq, planes,
                   pl.BlockSpec((up_rows, w_up.shape[2]), lambda i: (i, 0)),
                   pl.BlockSpec((down_rows, w_down.shape[2]), lambda i: (i, 0)),
                   pl.BlockSpec((gate_rows, w_gate.shape[2]), lambda i: (i, 0))],
        out_shape=[jax.ShapeDtypeStruct((t, d), F32)]
                  + [jax.ShapeDtypeStruct(b.shape, F32) for b in (cache_kt, cache_vt, state_t)]
                  + [jax.ShapeDtypeStruct(m.shape[1:], BF16) for m in (w_up, w_down, w_gate)],
        input_output_aliases={first_window + i: 1 + i for i in range(len(windows))},
        scratch_shapes=[pltpu.VMEM((t, D_ATTN), F32), pltpu.VMEM((t, D_KV), F32), pltpu.VMEM((t, D_KV), F32), slabs,
                        pltpu.VMEM((t, D_ATTN), F32), slabs],
        compiler_params=pltpu.CompilerParams(
            dimension_semantics=("arbitrary",), vmem_limit_bytes=VMEM_LIMIT),
        name=f"sample_mixer_l{layer}",
    )(*args)


def _rope_tables(pos):
    half = HEAD_DIM // 2
    inv = 1.0 / (ROPE_THETA ** (jnp.arange(half, dtype=F32) / half))
    ang = pos.astype(F32)[:, None] * inv[None, :]
    cos, sin = jnp.cos(ang), jnp.sin(ang)
    return jnp.tile(cos, (1, 4)), jnp.tile(jnp.concatenate([-sin, sin], axis=1), (1, 2))


def kernel(x_prompt, x_sample, p_prompt, p_sample, cache_k, cache_v, state_conv, g_mix, w_in, sinks, w_dw, b_dw,
           ln_g, ln_b, w_pw2, g_attn_out, g_conv_out, w_o, g_ffn, w_up, w_down, g_ple, w_ple_gate, w_ple, g_final):
    depth = w_in.shape[0]
    bp, sp, d = x_prompt.shape
    bs, ts, _ = x_sample.shape
    vec = lambda a: a.reshape(depth, 1, a.shape[-1])
    w = dict(
        sinks=sinks, g_mix=vec(g_mix), w_in=w_in.astype(BF16), w_dw=w_dw, b_dw=vec(b_dw), ln_g=vec(ln_g),
        ln_b=vec(ln_b), w_pw2=w_pw2.astype(BF16), g_attn_out=vec(g_attn_out), g_conv_out=vec(g_conv_out),
        w_o=w_o.astype(BF16), g_ffn=vec(g_ffn), w_up_f32=w_up, w_down_f32=w_down,
        g_ple=vec(g_ple), w_gate_f32=w_ple_gate, w_ple=w_ple.astype(BF16))
    g_fin = g_final.reshape(1, d)

    cos_p, sin_p = _rope_tables(jnp.arange(sp, dtype=jnp.int32))
    pos_s = PAST_LEN + (jnp.arange(bs * ts, dtype=jnp.int32) % ts)
    cos_s, sin_s = _rope_tables(pos_s)

    pp = p_prompt.reshape(depth, bp * sp, -1)
    ps = p_sample.reshape(depth, bs * ts, -1)
    feature_major = lambda c: c.transpose(0, 1, 3, 4, 2).reshape(depth, bs, D_KV, WINDOW)
    ckt, cvt = feature_major(cache_k), feature_major(cache_v)
    st = state_conv.transpose(0, 2, 1, 3)
    windows = ()

    yp = x_prompt
    ys = x_sample.reshape(bs * ts, d)
    kp, vp, up = [], [], []
    for i in range(depth):
        last = g_fin if i == depth - 1 else None
        ys, *windows, w_up_i, w_down_i, w_gate_i = _sample_mixer(i, ts, ys, cos_s, sin_s, ckt, cvt, st, windows, w)

        yp, ys, kt, vt, ut = _layer(i, yp, cos_p, sin_p, pp, ys, ps, w, w_up_i, w_down_i, w_gate_i, g_final=last)
        yp = yp.reshape(bp, sp, d)
        kp.append(kt); vp.append(vt); up.append(ut[:, CONV_PAD - CONV_HIST:, :])

    heads = lambda lst, b: jnp.stack(lst).reshape(depth, b, WINDOW, N_KV_HEADS, HEAD_DIM)
    window_major = lambda c: c.reshape(depth, bs, N_KV_HEADS, HEAD_DIM, WINDOW).transpose(0, 1, 4, 2, 3)
    kw, vw, cs = windows
    return (yp, ys.reshape(bs, ts, d), heads(kp, bp), heads(vp, bp), jnp.stack(up),
            window_major(kw), window_major(vw), cs.transpose(0, 2, 1, 3))
```
